```python
import math
import jax, jax.numpy as jnp
from jax import lax
import numpy as np

D_MODEL = 1024
BATCH = 8
SEQ = 2048
DEPTH = 1
DEC_BATCH = 128
DEC_SEQ = 8
PAST_LEN = 16384
PAGE_SIZE = 128

N_META = 16
D_MIX = D_MODEL
D_CONV = D_MIX // 2
D_SSM = D_MIX - D_CONV
CONV_K = 31
SSM_GC = 16
SSM_GROUPS = D_SSM // SSM_GC
SSM_P = 64
D_FF = 2816
RMS_EPS = 1e-6
LN_EPS = 1e-5

kernel_name = "hymba_conformer_s5_macaron_step"


def _rmsnorm(x, g):
    xf = x.astype(jnp.float32)
    y = xf * lax.rsqrt(jnp.mean(xf * xf, axis=-1, keepdims=True) + RMS_EPS)
    return (y * g.astype(jnp.float32)).astype(x.dtype)


def _swiglu(h, w_gate, w_up, w_down):
    return (jax.nn.silu(h @ w_gate) * (h @ w_up)) @ w_down


def _conv_group(v, gate, buf, conv_w, conv_b, ln_g, ln_b):
    u = v * jax.nn.sigmoid(gate)
    ext = jnp.concatenate([buf.astype(u.dtype), u], axis=1)
    y = lax.conv_general_dilated(
        ext, conv_w.astype(u.dtype)[:, None, :], window_strides=(1,), padding="VALID",
        dimension_numbers=("NWC", "WIO", "NWC"), feature_group_count=D_CONV)
    y = y + conv_b.astype(y.dtype)
    yf = y.astype(jnp.float32)
    mu = jnp.mean(yf, axis=-1, keepdims=True)
    var = jnp.mean(jnp.square(yf - mu), axis=-1, keepdims=True)
    yf = (yf - mu) * lax.rsqrt(var + LN_EPS) * ln_g.astype(jnp.float32) + ln_b.astype(jnp.float32)
    out = jax.nn.silu(yf).astype(u.dtype)
    new_buf = ext[:, -(CONV_K - 1):, :]
    return out, new_buf


def _ssm_combine(e1, e2):
    a1, b1 = e1
    a2, b2 = e2
    return a1 * a2, a2 * b1 + b2


def _ssm_group(u, h0_re, h0_im, lam_re, lam_im, log_dt, b_re, b_im, c_re, c_im, d, glu_w, glu_b):
    bsz, seqlen, _ = u.shape
    f32 = jnp.float32
    uf = u.astype(f32).reshape(bsz, seqlen, SSM_GROUPS, SSM_GC)
    lam = lax.complex(lam_re.astype(f32), lam_im.astype(f32))
    dt = jnp.exp(log_dt.astype(f32))[:, None]
    lam_bar = jnp.exp(lam * dt)
    b_mat = lax.complex(b_re.astype(f32), b_im.astype(f32))
    b_bar = ((lam_bar - 1.0) / lam)[..., None] * b_mat
    c_mat = lax.complex(c_re.astype(f32), c_im.astype(f32))
    bu = jnp.einsum("gpc,blgc->blgp", b_bar, uf.astype(jnp.complex64))
    h0 = lax.complex(h0_re.astype(f32), h0_im.astype(f32))
    bu = bu.at[:, 0].add(lam_bar[None] * h0)
    a = jnp.broadcast_to(lam_bar, bu.shape)
    _, h = lax.associative_scan(_ssm_combine, (a, bu), axis=1)
    y = jnp.real(jnp.einsum("gcp,blgp->blgc", c_mat, h)) + d.astype(f32) * uf
    y = y.reshape(bsz, seqlen, D_SSM)
    z = jax.nn.gelu(y)
    out = z * jax.nn.sigmoid(z @ glu_w.astype(f32) + glu_b.astype(f32))
    h_last = h[:, -1]
    return out.astype(u.dtype), jnp.real(h_last), jnp.imag(h_last)


def _layer(x, conv_buf, h_re, h_im, p):
    h = _rmsnorm(x, p["ffn1_norm"])
    x = x + 0.5 * _swiglu(h, p["ffn1_w_gate"], p["ffn1_w_up"], p["ffn1_w_down"])
    h = _rmsnorm(x, p["mix_norm"])
    proj = h @ p["w_in"]
    cv = proj[..., :D_CONV]
    cg = proj[..., D_CONV:2 * D_CONV]
    su = proj[..., 2 * D_CONV:]
    yc, new_buf = _conv_group(cv, cg, conv_buf, p["conv_w"], p["conv_b"], p["conv_ln_g"], p["conv_ln_b"])
    ys, nh_re, nh_im = _ssm_group(su, h_re, h_im, p["ssm_lambda_re"], p["ssm_lambda_im"], p["ssm_log_dt"],
                                  p["ssm_b_re"], p["ssm_b_im"], p["ssm_c_re"], p["ssm_c_im"], p["ssm_d"],
                                  p["ssm_glu_w"], p["ssm_glu_b"])
    x = x + jnp.concatenate([yc, ys], axis=-1) @ p["w_out"]
    h = _rmsnorm(x, p["ffn2_norm"])
    x = x + 0.5 * _swiglu(h, p["ffn2_w_gate"], p["ffn2_w_up"], p["ffn2_w_down"])
    return x, new_buf, nh_re, nh_im


def setup_inputs(seed: int = 0) -> dict:
    key = jax.random.key(seed)
    ks = jax.random.split(key, 40)
    f32 = jnp.float32
    nrm = lambda k, shape, s: jax.random.normal(k, shape, f32) * s
    n_idx = jnp.arange(SSM_P, dtype=f32)
    lam_im = jnp.broadcast_to(math.pi * n_idx, (DEPTH, SSM_GROUPS, SSM_P))
    return {
        "x_prompt": nrm(ks[0], (BATCH, SEQ, D_MODEL), 1.0),
        "x_sample": nrm(ks[1], (DEC_BATCH, DEC_SEQ, D_MODEL), 1.0),
        "state_conv": nrm(ks[2], (DEPTH, DEC_BATCH, CONV_K - 1, D_CONV), 1.0),
        "state_ssm_re": nrm(ks[3], (DEPTH, DEC_BATCH, SSM_GROUPS, SSM_P), 0.1),
        "state_ssm_im": nrm(ks[4], (DEPTH, DEC_BATCH, SSM_GROUPS, SSM_P), 0.1),
        "meta_tokens": nrm(ks[5], (N_META, D_MODEL), 1.0),
        "ffn1_norm": 1.0 + nrm(ks[6], (DEPTH, D_MODEL), 0.02),
        "ffn1_w_gate": nrm(ks[7], (DEPTH, D_MODEL, D_FF), D_MODEL ** -0.5),
        "ffn1_w_up": nrm(ks[8], (DEPTH, D_MODEL, D_FF), D_MODEL ** -0.5),
        "ffn1_w_down": nrm(ks[9], (DEPTH, D_FF, D_MODEL), D_FF ** -0.5),
        "mix_norm": 1.0 + nrm(ks[10], (DEPTH, D_MODEL), 0.02),
        "w_in": nrm(ks[11], (DEPTH, D_MODEL, 2 * D_CONV + D_SSM), D_MODEL ** -0.5),
        "conv_w": nrm(ks[12], (DEPTH, CONV_K, D_CONV), CONV_K ** -0.5),
        "conv_b": nrm(ks[13], (DEPTH, D_CONV), 0.02),
        "conv_ln_g": 1.0 + nrm(ks[14], (DEPTH, D_CONV), 0.02),
        "conv_ln_b": nrm(ks[15], (DEPTH, D_CONV), 0.02),
        "ssm_lambda_re": -0.5 + nrm(ks[16], (DEPTH, SSM_GROUPS, SSM_P), 0.01),
        "ssm_lambda_im": lam_im + nrm(ks[17], (DEPTH, SSM_GROUPS, SSM_P), 0.01),
        "ssm_log_dt": jax.random.uniform(ks[18], (DEPTH, SSM_GROUPS), f32, math.log(1e-3), math.log(1e-1)),
        "ssm_b_re": nrm(ks[19], (DEPTH, SSM_GROUPS, SSM_P, SSM_GC), (2.0 * SSM_GC) ** -0.5),
        "ssm_b_im": nrm(ks[20], (DEPTH, SSM_GROUPS, SSM_P, SSM_GC), (2.0 * SSM_GC) ** -0.5),
        "ssm_c_re": nrm(ks[21], (DEPTH, SSM_GROUPS, SSM_GC, SSM_P), (2.0 * SSM_P) ** -0.5),
        "ssm_c_im": nrm(ks[22], (DEPTH, SSM_GROUPS, SSM_GC, SSM_P), (2.0 * SSM_P) ** -0.5),
        "ssm_d": nrm(ks[23], (DEPTH, SSM_GROUPS, SSM_GC), 1.0),
        "ssm_glu_w": nrm(ks[24], (DEPTH, D_SSM, D_SSM), D_SSM ** -0.5),
        "ssm_glu_b": nrm(ks[25], (DEPTH, D_SSM), 0.02),
        "w_out": nrm(ks[26], (DEPTH, D_MIX, D_MODEL), D_MIX ** -0.5),
        "ffn2_norm": 1.0 + nrm(ks[27], (DEPTH, D_MODEL), 0.02),
        "ffn2_w_gate": nrm(ks[28], (DEPTH, D_MODEL, D_FF), D_MODEL ** -0.5),
        "ffn2_w_up": nrm(ks[29], (DEPTH, D_MODEL, D_FF), D_MODEL ** -0.5),
        "ffn2_w_down": nrm(ks[30], (DEPTH, D_FF, D_MODEL), D_FF ** -0.5),
        "final_norm": 1.0 + nrm(ks[31], (D_MODEL,), 0.02),
    }


def reference(x_prompt, x_sample, state_conv, state_ssm_re, state_ssm_im, meta_tokens,
              ffn1_norm, ffn1_w_gate, ffn1_w_up, ffn1_w_down, mix_norm, w_in,
              conv_w, conv_b, conv_ln_g, conv_ln_b,
              ssm_lambda_re, ssm_lambda_im, ssm_log_dt, ssm_b_re, ssm_b_im, ssm_c_re, ssm_c_im, ssm_d,
              ssm_glu_w, ssm_glu_b, w_out, ffn2_norm, ffn2_w_gate, ffn2_w_up, ffn2_w_down, final_norm):
    bp = x_prompt.shape[0]
    meta = jnp.broadcast_to(meta_tokens.astype(x_prompt.dtype)[None], (bp, N_META, D_MODEL))
    xp = jnp.concatenate([meta, x_prompt], axis=1)
    xs = x_sample
    conv_p_list, re_p_list, im_p_list = [], [], []
    conv_s_list, re_s_list, im_s_list = [], [], []
    for l in range(DEPTH):
        p = {
            "ffn1_norm": ffn1_norm[l], "ffn1_w_gate": ffn1_w_gate[l], "ffn1_w_up": ffn1_w_up[l],
            "ffn1_w_down": ffn1_w_down[l], "mix_norm": mix_norm[l], "w_in": w_in[l],
            "conv_w": conv_w[l], "conv_b": conv_b[l], "conv_ln_g": conv_ln_g[l], "conv_ln_b": conv_ln_b[l],
            "ssm_lambda_re": ssm_lambda_re[l], "ssm_lambda_im": ssm_lambda_im[l], "ssm_log_dt": ssm_log_dt[l],
            "ssm_b_re": ssm_b_re[l], "ssm_b_im": ssm_b_im[l], "ssm_c_re": ssm_c_re[l], "ssm_c_im": ssm_c_im[l],
            "ssm_d": ssm_d[l], "ssm_glu_w": ssm_glu_w[l], "ssm_glu_b": ssm_glu_b[l], "w_out": w_out[l],
            "ffn2_norm": ffn2_norm[l], "ffn2_w_gate": ffn2_w_gate[l], "ffn2_w_up": ffn2_w_up[l],
            "ffn2_w_down": ffn2_w_down[l],
        }
        zero_buf = jnp.zeros((bp, CONV_K - 1, D_CONV), xp.dtype)
        zero_h = jnp.zeros((bp, SSM_GROUPS, SSM_P), jnp.float32)
        xp, cbp, hrp, hip = _layer(xp, zero_buf, zero_h, zero_h, p)
        xs, cbs, hrs, his = _layer(xs, state_conv[l], state_ssm_re[l], state_ssm_im[l], p)
        conv_p_list.append(cbp); re_p_list.append(hrp); im_p_list.append(hip)
        conv_s_list.append(cbs); re_s_list.append(hrs); im_s_list.append(his)
    y_prompt = _rmsnorm(xp, final_norm)[:, N_META:, :]
    y_sample = _rmsnorm(xs, final_norm)
    new_conv_prompt = jnp.stack(conv_p_list, axis=0)
    new_ssm_re_prompt = jnp.stack(re_p_list, axis=0)
    new_ssm_im_prompt = jnp.stack(im_p_list, axis=0)
    new_conv_sample = jnp.stack(conv_s_list, axis=0)
    new_ssm_re_sample = jnp.stack(re_s_list, axis=0)
    new_ssm_im_sample = jnp.stack(im_s_list, axis=0)
    return (y_prompt, y_sample, new_conv_prompt, new_ssm_re_prompt, new_ssm_im_prompt,
            new_conv_sample, new_ssm_re_sample, new_ssm_im_sample)
```

```python
import functools

import jax
import jax.numpy as jnp
from jax import lax
from jax.experimental import pallas as pl
from jax.experimental.pallas import tpu as pltpu

F32 = jnp.float32
BF16 = jnp.bfloat16

D_MODEL = 1024
D_FF = 2816
D_CONV = 512
D_SSM = 512
CONV_K = 31
CONV_TAIL = CONV_K - 1
N_META = 16
SSM_GROUPS = 32
SSM_GC = 16
SSM_P = 64
N_STATE = SSM_GROUPS * SSM_P
RMS_EPS = 1e-6
LN_EPS = 1e-5

SUBLANES = 8
LANES = 128
FF_CHUNK = 256
N_FF_CHUNKS = D_FF // FF_CHUNK
SSM_BLOCKS = D_SSM // LANES
BLOCK_STATES = N_STATE // SSM_BLOCKS
VMEM_LIMIT_BYTES = 56 * 1024 * 1024


def _const_spec(shape):
    zeros = (0,) * len(shape)
    return pl.BlockSpec(shape, lambda *_: zeros, pipeline_mode=pl.Buffered(1))


def _rmsnorm(x, g):
    return x * lax.rsqrt(jnp.mean(x * x, axis=-1, keepdims=True) + RMS_EPS) * g


def _swiglu(hb, wg_ref, wu_ref, wd_ref):
    acc = jnp.zeros((hb.shape[0], D_MODEL), F32)
    for c in range(N_FF_CHUNKS):
        g = jnp.dot(hb, wg_ref[c], preferred_element_type=F32)
        u = jnp.dot(hb, wu_ref[c], preferred_element_type=F32)
        a = (g * jax.nn.sigmoid(g) * u).astype(BF16)
        acc = acc + jnp.dot(a, wd_ref[c], preferred_element_type=F32)
    return acc


def _ffn_in_kernel(x_ref, n1_ref, wg_ref, wu_ref, wd_ref, n2_ref, win_ref,
                   x1_ref, uc_ref, su_ref):
    x = x_ref[...]
    hb = _rmsnorm(x, n1_ref[...]).astype(BF16)
    x1 = x + 0.5 * _swiglu(hb, wg_ref, wu_ref, wd_ref)
    x1_ref[...] = x1
    h2 = _rmsnorm(x1, n2_ref[...]).astype(BF16)
    proj = jnp.dot(h2, win_ref[...], preferred_element_type=F32)
    cv = proj[:, :D_CONV]
    cg = proj[:, D_CONV:2 * D_CONV]
    uc_ref[...] = cv * jax.nn.sigmoid(cg)
    su_ref[...] = proj[:, 2 * D_CONV:]


def _ffn_in(x, n1, wg, wu, wd, n2, win, tm):
    rows = x.shape[0]
    assert rows % tm == 0
    row_spec = lambda w: pl.BlockSpec((tm, w), lambda i: (i, 0))
    return pl.pallas_call(
        _ffn_in_kernel,
        grid=(rows // tm,),
        in_specs=[row_spec(D_MODEL), _const_spec(n1.shape), _const_spec(wg.shape),
                  _const_spec(wu.shape), _const_spec(wd.shape), _const_spec(n2.shape),
                  _const_spec(win.shape)],
        out_specs=[row_spec(D_MODEL), row_spec(D_CONV), row_spec(D_SSM)],
        out_shape=[jax.ShapeDtypeStruct((rows, D_MODEL), F32),
                   jax.ShapeDtypeStruct((rows, D_CONV), F32),
                   jax.ShapeDtypeStruct((rows, D_SSM), F32)],
        compiler_params=pltpu.CompilerParams(dimension_semantics=("parallel",),
                                             vmem_limit_bytes=VMEM_LIMIT_BYTES),
        name="ffn_in",
    )(x, n1, wg, wu, wd, n2, win)


def _ffn_out_kernel(x_ref, n1_ref, wg_ref, wu_ref, wd_ref, nf_ref, y_ref):
    x = x_ref[...]
    hb = _rmsnorm(x, n1_ref[...]).astype(BF16)
    x3 = x + 0.5 * _swiglu(hb, wg_ref, wu_ref, wd_ref)
    y_ref[...] = _rmsnorm(x3, nf_ref[...])


def _ffn_out(x, n1, wg, wu, wd, nf, tm):
    rows = x.shape[0]
    assert rows % tm == 0
    row_spec = pl.BlockSpec((tm, D_MODEL), lambda i: (i, 0))
    return pl.pallas_call(
        _ffn_out_kernel,
        grid=(rows // tm,),
        in_specs=[row_spec, _const_spec(n1.shape), _const_spec(wg.shape), _const_spec(wu.shape),
                  _const_spec(wd.shape), _const_spec(nf.shape)],
        out_specs=row_spec,
        out_shape=jax.ShapeDtypeStruct((rows, D_MODEL), F32),
        compiler_params=pltpu.CompilerParams(dimension_semantics=("parallel",),
                                             vmem_limit_bytes=VMEM_LIMIT_BYTES),
        name="ffn_out",
    )(x, n1, wg, wu, wd, nf)


def _ssm_param_kernel(lre_ref, lim_ref, ldt_ref, bre_ref, bim_ref, are_ref, aim_ref, wb_ref):
    lre = lre_ref[...]
    lim = lim_ref[...]
    dt = jnp.exp(ldt_ref[...])
    mag = jnp.exp(lre * dt)
    abr = mag * jnp.cos(lim * dt)
    abi = mag * jnp.sin(lim * dt)
    are_ref[...] = abr
    aim_ref[...] = abi
    nr = abr - 1.0
    den = lre * lre + lim * lim
    cr = (nr * lre + abi * lim) / den
    ci = (abi * lre - nr * lim) / den
    for c in range(SSM_BLOCKS):
        sl = slice(c * BLOCK_STATES, (c + 1) * BLOCK_STATES)
        crc, cic = cr[:, sl], ci[:, sl]
        br, bi = bre_ref[c], bim_ref[c]
        wb_ref[c, :, :BLOCK_STATES] = (crc * br - cic * bi).astype(BF16)
        wb_ref[c, :, BLOCK_STATES:] = (crc * bi + cic * br).astype(BF16)


def _ssm_params(lre, lim, ldt, bre, bim):
    return pl.pallas_call(
        _ssm_param_kernel,
        out_shape=[jax.ShapeDtypeStruct((1, N_STATE), F32),
                   jax.ShapeDtypeStruct((1, N_STATE), F32),
                   jax.ShapeDtypeStruct((SSM_BLOCKS, LANES, 2 * BLOCK_STATES), BF16)],
        name="ssm_params",
    )(lre, lim, ldt, bre, bim)


def _mixer_kernel(uc_ref, su_ref, x1_ref, tail0_ref, h0re_ref, h0im_ref,
                  cw_ref, cb_ref, lg_ref, lb_ref, are_ref, aim_ref, wb_ref, wcre_ref, wcim_ref,
                  d_ref, gw_ref, gb_ref, wo_ref,
                  x2_ref, tail_ref, hre_ref, him_ref,
                  ext_ref, hbuf_ref, yc_ref, ys_ref, *, bb, steps):
    rows = steps * bb
    tail_rows = CONV_TAIL * bb

    @pl.when(pl.program_id(1) == 0)
    def _():
        tail_ref[...] = tail0_ref[...]
        hre_ref[...] = h0re_ref[...]
        him_ref[...] = h0im_ref[...]

    ext_ref[0:tail_rows, :] = tail_ref[...]
    ext_ref[tail_rows:tail_rows + rows, :] = uc_ref[...]

    def conv_body(r, carry):
        o = pl.multiple_of(r * SUBLANES, SUBLANES)
        acc = jnp.zeros((SUBLANES, D_CONV), F32)
        for k in range(CONV_K):
            acc = acc + cw_ref[k:k + 1, :] * ext_ref[pl.ds(o + k * bb, SUBLANES), :]
        acc = acc + cb_ref[...]
        mu = jnp.mean(acc, axis=-1, keepdims=True)
        cen = acc - mu
        var = jnp.mean(cen * cen, axis=-1, keepdims=True)
        yn = cen * lax.rsqrt(var + LN_EPS) * lg_ref[...] + lb_ref[...]
        yc_ref[pl.ds(o, SUBLANES), :] = yn * jax.nn.sigmoid(yn)
        return carry

    lax.fori_loop(0, rows // SUBLANES, conv_body, 0)
    tail_ref[...] = ext_ref[rows:rows + tail_rows, :]

    su = su_ref[...]
    sub = su.astype(BF16)
    for c in range(SSM_BLOCKS):
        st = slice(c * BLOCK_STATES, (c + 1) * BLOCK_STATES)
        hbuf_ref[...] = jnp.dot(sub[:, c * LANES:(c + 1) * LANES], wb_ref[c],
                                preferred_element_type=F32)
        ar = jnp.broadcast_to(are_ref[:, st], (SUBLANES, BLOCK_STATES))
        ai = jnp.broadcast_to(aim_ref[:, st], (SUBLANES, BLOCK_STATES))

        def seq_body(rb, carry, st=st, ar=ar, ai=ai):
            ro = pl.multiple_of(rb * SUBLANES, SUBLANES)

            def step_body(t, h):
                hr, hi = h
                o = pl.multiple_of(t * bb + ro, SUBLANES)
                nr = ar * hr - ai * hi + hbuf_ref[pl.ds(o, SUBLANES), :BLOCK_STATES]
                ni = ar * hi + ai * hr + hbuf_ref[pl.ds(o, SUBLANES), BLOCK_STATES:]
                hbuf_ref[pl.ds(o, SUBLANES), :BLOCK_STATES] = nr
                hbuf_ref[pl.ds(o, SUBLANES), BLOCK_STATES:] = ni
                return nr, ni

            h0 = (hre_ref[pl.ds(ro, SUBLANES), st], him_ref[pl.ds(ro, SUBLANES), st])
            hr, hi = lax.fori_loop(0, steps, step_body, h0, unroll=2)
            hre_ref[pl.ds(ro, SUBLANES), st] = hr
            him_ref[pl.ds(ro, SUBLANES), st] = hi
            return carry

        lax.fori_loop(0, bb // SUBLANES, seq_body, 0)
        hb = hbuf_ref[...].astype(BF16)
        ys_ref[:, c * LANES:(c + 1) * LANES] = (
            jnp.dot(hb[:, :BLOCK_STATES], wcre_ref[c], preferred_element_type=F32)
            - jnp.dot(hb[:, BLOCK_STATES:], wcim_ref[c], preferred_element_type=F32))

    z = jax.nn.gelu(ys_ref[...] + d_ref[...] * su)
    gate = jnp.dot(z.astype(BF16), gw_ref[...], preferred_element_type=F32) + gb_ref[...]
    outs = z * jax.nn.sigmoid(gate)

    mix = (jnp.dot(yc_ref[...].astype(BF16), wo_ref[:D_CONV, :], preferred_element_type=F32)
           + jnp.dot(outs.astype(BF16), wo_ref[D_CONV:, :], preferred_element_type=F32))
    x2_ref[...] = x1_ref[...] + mix


def _mixer(uc, su, x1, tail0, h0re, h0im, consts, *, groups, bb, steps, chunks):
    rows = steps * bb
    tail_rows = CONV_TAIL * bb
    assert uc.shape[0] == groups * chunks * rows and bb % SUBLANES == 0
    row_spec = lambda w: pl.BlockSpec((rows, w), lambda g, i: (g * chunks + i, 0))
    grp_spec = lambda r, w: pl.BlockSpec((r, w), lambda g, i: (g, 0))
    return pl.pallas_call(
        functools.partial(_mixer_kernel, bb=bb, steps=steps),
        grid=(groups, chunks),
        in_specs=[row_spec(D_CONV), row_spec(D_SSM), row_spec(D_MODEL),
                  grp_spec(tail_rows, D_CONV), grp_spec(bb, N_STATE), grp_spec(bb, N_STATE)]
                 + [_const_spec(c.shape) for c in consts],
        out_specs=[row_spec(D_MODEL), grp_spec(tail_rows, D_CONV),
                   grp_spec(bb, N_STATE), grp_spec(bb, N_STATE)],
        out_shape=[jax.ShapeDtypeStruct((groups * chunks * rows, D_MODEL), F32),
                   jax.ShapeDtypeStruct((groups * tail_rows, D_CONV), F32),
                   jax.ShapeDtypeStruct((groups * bb, N_STATE), F32),
                   jax.ShapeDtypeStruct((groups * bb, N_STATE), F32)],
        scratch_shapes=[pltpu.VMEM((rows + tail_rows, D_CONV), F32),
                        pltpu.VMEM((rows, 2 * BLOCK_STATES), F32),
                        pltpu.VMEM((rows, D_CONV), F32),
                        pltpu.VMEM((rows, D_SSM), F32)],
        compiler_params=pltpu.CompilerParams(dimension_semantics=("parallel", "arbitrary"),
                                             vmem_limit_bytes=VMEM_LIMIT_BYTES),
        name="mixer",
    )(uc, su, x1, tail0, h0re, h0im, *consts)


def _ff_chunks_in(w):
    return w.reshape(D_MODEL, N_FF_CHUNKS, FF_CHUNK).transpose(1, 0, 2).astype(BF16)


def _ff_chunks_out(w):
    return w.reshape(N_FF_CHUNKS, FF_CHUNK, D_MODEL).astype(BF16)


def _block_diag(blocks):
    nb, n, a, b = blocks.shape
    eye = jnp.eye(n, dtype=blocks.dtype)
    return jnp.einsum("cgab,gh->cgahb", blocks, eye).reshape(nb, n * a, n * b)


def _to_steps_major(x, groups):
    b, l, w = x.shape
    return x.reshape(groups, b // groups, l, w).transpose(0, 2, 1, 3).reshape(b * l, w)


def _from_steps_major(x, groups, batch):
    rows, w = x.shape
    l = rows // batch
    return x.reshape(groups, l, batch // groups, w).transpose(0, 2, 1, 3).reshape(batch, l, w)


PROMPT_STEPS = 86
SAMPLE_GROUPS = 4


def kernel(x_prompt, x_sample, state_conv, state_ssm_re, state_ssm_im, meta_tokens, ffn1_norm, ffn1_w_gate, ffn1_w_up, ffn1_w_down, mix_norm, w_in, conv_w, conv_b, conv_ln_g, conv_ln_b, ssm_lambda_re, ssm_lambda_im, ssm_log_dt, ssm_b_re, ssm_b_im, ssm_c_re, ssm_c_im, ssm_d, ssm_glu_w, ssm_glu_b, w_out, ffn2_norm, ffn2_w_gate, ffn2_w_up, ffn2_w_down, final_norm):
    assert ffn1_norm.shape[0] == 1, "one layer"
    bp, lp, _ = x_prompt.shape
    bs, ls, _ = x_sample.shape
    lpm = lp + N_META
    groups_per_block = SSM_GROUPS // SSM_BLOCKS

    row = lambda v: v.reshape(1, -1).astype(F32)
    w1g, w1u, w1d = _ff_chunks_in(ffn1_w_gate[0]), _ff_chunks_in(ffn1_w_up[0]), _ff_chunks_out(ffn1_w_down[0])
    w2g, w2u, w2d = _ff_chunks_in(ffn2_w_gate[0]), _ff_chunks_in(ffn2_w_up[0]), _ff_chunks_out(ffn2_w_down[0])
    win = w_in[0].astype(BF16)
    b_blocks = lambda b: _block_diag(
        b.reshape(SSM_BLOCKS, groups_per_block, SSM_P, SSM_GC).transpose(0, 1, 3, 2))
    c_blocks = lambda c: _block_diag(
        c.reshape(SSM_BLOCKS, groups_per_block, SSM_GC, SSM_P).transpose(0, 1, 3, 2)).astype(BF16)
    are, aim, wb = _ssm_params(row(ssm_lambda_re[0]), row(ssm_lambda_im[0]),
                               row(jnp.repeat(ssm_log_dt[0], SSM_P)),
                               b_blocks(ssm_b_re[0]), b_blocks(ssm_b_im[0]))
    consts = (conv_w[0], row(conv_b[0]), row(conv_ln_g[0]), row(conv_ln_b[0]), are, aim, wb,
              c_blocks(ssm_c_re[0]), c_blocks(ssm_c_im[0]), row(ssm_d[0]),
              ssm_glu_w[0].astype(BF16), row(ssm_glu_b[0]), w_out[0].astype(BF16))

    def layer(x_rows, tail0, h0re, h0im, *, tm, groups, bb, steps, chunks):
        x1, uc, su = _ffn_in(x_rows, row(ffn1_norm[0]), w1g, w1u, w1d, row(mix_norm[0]), win, tm)
        x2, tail, hre, him = _mixer(uc, su, x1, tail0, h0re, h0im, consts,
                                    groups=groups, bb=bb, steps=steps, chunks=chunks)
        y = _ffn_out(x2, row(ffn2_norm[0]), w2g, w2u, w2d, row(final_norm), tm)
        return y, tail, hre, him

    meta = jnp.broadcast_to(meta_tokens.astype(x_prompt.dtype)[None], (bp, N_META, D_MODEL))
    xp = _to_steps_major(jnp.concatenate([meta, x_prompt], axis=1), 1)
    assert lpm % PROMPT_STEPS == 0 and bp % SUBLANES == 0
    yp, tail_p, hre_p, him_p = layer(
        xp, jnp.zeros((CONV_TAIL * bp, D_CONV), F32), jnp.zeros((bp, N_STATE), F32),
        jnp.zeros((bp, N_STATE), F32),
        tm=PROMPT_STEPS * bp, groups=1, bb=bp, steps=PROMPT_STEPS, chunks=lpm // PROMPT_STEPS)

    xs = _to_steps_major(x_sample, SAMPLE_GROUPS)
    ys, tail_s, hre_s, him_s = layer(
        xs, _to_steps_major(state_conv[0], SAMPLE_GROUPS),
        state_ssm_re[0].reshape(bs, N_STATE), state_ssm_im[0].reshape(bs, N_STATE),
        tm=bs * ls // 2, groups=SAMPLE_GROUPS, bb=bs // SAMPLE_GROUPS, steps=ls, chunks=1)

    y_prompt = _from_steps_major(yp, 1, bp)[:, N_META:, :]
    y_sample = _from_steps_major(ys, SAMPLE_GROUPS, bs)
    state = lambda h, b: h.reshape(1, b, SSM_GROUPS, SSM_P)
    return (y_prompt, y_sample,
            _from_steps_major(tail_p, 1, bp)[None], state(hre_p, bp), state(him_p, bp),
            _from_steps_major(tail_s, SAMPLE_GROUPS, bs)[None], state(hre_s, bs), state(him_s, bs))
```

```python
import functools

import jax
import jax.numpy as jnp
from jax import lax
from jax.experimental import pallas as pl
from jax.experimental.pallas import tpu as pltpu

F32 = jnp.float32
BF16 = jnp.bfloat16

D_MODEL = 1024
D_FF = 2816
D_CONV = 512
D_SSM = 512
CONV_K = 31
CONV_TAIL = CONV_K - 1
N_META = 16
SSM_GROUPS = 32
SSM_GC = 16
SSM_P = 64
N_STATE = SSM_GROUPS * SSM_P
RMS_EPS = 1e-6
LN_EPS = 1e-5

SUBLANES = 8
LANES = 128
FF_CHUNK = 256
N_FF_CHUNKS = D_FF // FF_CHUNK
CONV_BLOCKS = D_CONV // LANES
SSM_BLOCKS = D_SSM // LANES
MODEL_BLOCKS = D_MODEL // LANES
BLOCK_STATES = N_STATE // SSM_BLOCKS
CONV_ROW_GROUP = 8 * SUBLANES
VMEM_LIMIT_BYTES = 56 * 1024 * 1024


def _const_spec(shape):
    zeros = (0,) * len(shape)
    return pl.BlockSpec(shape, lambda *_: zeros, pipeline_mode=pl.Buffered(1))


def _rmsnorm(x, g):
    return x * lax.rsqrt(jnp.mean(x * x, axis=-1, keepdims=True) + RMS_EPS) * g


def _swiglu(hb, wg_ref, wu_ref, wd_ref):
    acc = jnp.zeros((hb.shape[0], D_MODEL), F32)
    for c in range(N_FF_CHUNKS):
        cols = slice(c * FF_CHUNK, (c + 1) * FF_CHUNK)
        g = jnp.dot(hb, wg_ref[:, cols], preferred_element_type=F32)
        u = jnp.dot(hb, wu_ref[:, cols], preferred_element_type=F32)
        a = (g * jax.nn.sigmoid(g) * u).astype(BF16)
        acc = acc + jnp.dot(a, wd_ref[cols, :], preferred_element_type=F32)
    return acc


def _ffn_in_kernel(x_ref, n1_ref, wg_ref, wu_ref, wd_ref, n2_ref, win_ref,
                   x1_ref, uc_ref, su_ref):
    x = x_ref[...]
    hb = _rmsnorm(x, n1_ref[...]).astype(BF16)
    x1 = x + 0.5 * _swiglu(hb, wg_ref, wu_ref, wd_ref)
    x1_ref[...] = x1
    h2 = _rmsnorm(x1, n2_ref[...]).astype(BF16)
    proj = jnp.dot(h2, win_ref[...], preferred_element_type=F32)
    cv = proj[:, :D_CONV]
    cg = proj[:, D_CONV:2 * D_CONV]
    uc_ref[...] = cv * jax.nn.sigmoid(cg)
    su_ref[...] = proj[:, 2 * D_CONV:]


def _ffn_in(x, n1, wg, wu, wd, n2, win, tm):
    rows = x.shape[0]
    assert rows % tm == 0
    row_spec = lambda w: pl.BlockSpec((tm, w), lambda i: (i, 0))
    return pl.pallas_call(
        _ffn_in_kernel,
        grid=(rows // tm,),
        in_specs=[row_spec(D_MODEL), _const_spec(n1.shape), _const_spec(wg.shape),
                  _const_spec(wu.shape), _const_spec(wd.shape), _const_spec(n2.shape),
                  _const_spec(win.shape)],
        out_specs=[row_spec(D_MODEL), row_spec(D_CONV), row_spec(D_SSM)],
        out_shape=[jax.ShapeDtypeStruct((rows, D_MODEL), F32),
                   jax.ShapeDtypeStruct((rows, D_CONV), F32),
                   jax.ShapeDtypeStruct((rows, D_SSM), F32)],
        compiler_params=pltpu.CompilerParams(dimension_semantics=("parallel",),
                                             vmem_limit_bytes=VMEM_LIMIT_BYTES),
        name="ffn_in",
    )(x, n1, wg, wu, wd, n2, win)


def _ffn_out_kernel(x_ref, n1_ref, wg_ref, wu_ref, wd_ref, nf_ref, y_ref):
    x = x_ref[...]
    hb = _rmsnorm(x, n1_ref[...]).astype(BF16)
    x3 = x + 0.5 * _swiglu(hb, wg_ref, wu_ref, wd_ref)
    y_ref[...] = _rmsnorm(x3, nf_ref[...])


def _ffn_out(x, n1, wg, wu, wd, nf, tm):
    rows = x.shape[0]
    assert rows % tm == 0
    row_spec = pl.BlockSpec((tm, D_MODEL), lambda i: (i, 0))
    return pl.pallas_call(
        _ffn_out_kernel,
        grid=(rows // tm,),
        in_specs=[row_spec, _const_spec(n1.shape), _const_spec(wg.shape), _const_spec(wu.shape),
                  _const_spec(wd.shape), _const_spec(nf.shape)],
        out_specs=row_spec,
        out_shape=jax.ShapeDtypeStruct((rows, D_MODEL), F32),
        compiler_params=pltpu.CompilerParams(dimension_semantics=("parallel",),
                                             vmem_limit_bytes=VMEM_LIMIT_BYTES),
        name="ffn_out",
    )(x, n1, wg, wu, wd, nf)


def _ssm_param_kernel(lre_ref, lim_ref, ldt_ref, bre_ref, bim_ref, are_ref, aim_ref, wb_ref):
    lre = lre_ref[...]
    lim = lim_ref[...]
    dt = jnp.exp(ldt_ref[...])
    mag = jnp.exp(lre * dt)
    abr = mag * jnp.cos(lim * dt)
    abi = mag * jnp.sin(lim * dt)
    are_ref[...] = abr
    aim_ref[...] = abi
    nr = abr - 1.0
    den = lre * lre + lim * lim
    cr = (nr * lre + abi * lim) / den
    ci = (abi * lre - nr * lim) / den
    for c in range(SSM_BLOCKS):
        sl = slice(c * BLOCK_STATES, (c + 1) * BLOCK_STATES)
        crc, cic = cr[:, sl], ci[:, sl]
        br, bi = bre_ref[c], bim_ref[c]
        wb_ref[c, :, :BLOCK_STATES] = (crc * br - cic * bi).astype(BF16)
        wb_ref[c, :, BLOCK_STATES:] = (crc * bi + cic * br).astype(BF16)


def _ssm_params(lre, lim, ldt, bre, bim):
    return pl.pallas_call(
        _ssm_param_kernel,
        out_shape=[jax.ShapeDtypeStruct((1, N_STATE), F32),
                   jax.ShapeDtypeStruct((1, N_STATE), F32),
                   jax.ShapeDtypeStruct((SSM_BLOCKS, LANES, 2 * BLOCK_STATES), BF16)],
        name="ssm_params",
    )(lre, lim, ldt, bre, bim)


def _mixer_kernel(uc_ref, su_ref, x1_ref, cs0_ref, h0re_ref, h0im_ref,
                  cw_ref, cb_ref, lg_ref, lb_ref, are_ref, aim_ref, wb_ref, wcre_ref, wcim_ref,
                  d_ref, gw_ref, gb_ref, wo_ref,
                  x2_ref, cs_ref, hre_ref, him_ref,
                  ext_ref, sut_ref, hbuf_ref, yc_ref, mix_ref, *, bb, steps, chunks):
    rows = steps * bb
    tail_rows = CONV_TAIL * bb
    chunk = pl.program_id(1)

    @pl.when(chunk == 0)
    def _():
        hre_ref[...] = h0re_ref[...]
        him_ref[...] = h0im_ref[...]
        for b in range(bb):
            for j in range(CONV_BLOCKS):
                ext_ref[j, pl.ds(b, CONV_TAIL, stride=bb), :] = cs0_ref[b, :, j * LANES:(j + 1) * LANES]

    for b in range(bb):
        for j in range(CONV_BLOCKS):
            ext_ref[j, pl.ds(tail_rows + b, steps, stride=bb), :] = uc_ref[b, :, j * LANES:(j + 1) * LANES]
        for j in range(SSM_BLOCKS):
            sut_ref[j, pl.ds(b, steps, stride=bb), :] = su_ref[b, :, j * LANES:(j + 1) * LANES]

    def conv_body(r, carry):
        o = pl.multiple_of(r * CONV_ROW_GROUP, CONV_ROW_GROUP)
        for j in range(CONV_BLOCKS):
            cols = slice(j * LANES, (j + 1) * LANES)
            accs = [jnp.zeros((SUBLANES, LANES), F32)] * (CONV_ROW_GROUP // SUBLANES)
            for k in range(CONV_K):
                wk = cw_ref[k:k + 1, cols]
                accs = [a + wk * ext_ref[j, pl.ds(o + q * SUBLANES + k * bb, SUBLANES), :]
                        for q, a in enumerate(accs)]
            for q, a in enumerate(accs):
                yc_ref[j, pl.ds(o + q * SUBLANES, SUBLANES), :] = a + cb_ref[:, cols]
        return carry

    lax.fori_loop(0, rows // CONV_ROW_GROUP, conv_body, 0)

    @pl.when(chunk == chunks - 1)
    def _():
        if steps >= CONV_TAIL:
            cs_ref[...] = uc_ref[:, steps - CONV_TAIL:, :]
        else:
            cs_ref[:, :CONV_TAIL - steps, :] = cs0_ref[:, steps:, :]
            cs_ref[:, CONV_TAIL - steps:, :] = uc_ref[...]

    if chunks > 1:
        ext_ref[:, 0:tail_rows, :] = ext_ref[:, rows:rows + tail_rows, :]

    ycs = [yc_ref[j] for j in range(CONV_BLOCKS)]
    mu = sum(jnp.sum(y, axis=-1, keepdims=True) for y in ycs) * (1.0 / D_CONV)
    cens = [y - mu for y in ycs]
    var = sum(jnp.sum(c * c, axis=-1, keepdims=True) for c in cens) * (1.0 / D_CONV)
    inv = lax.rsqrt(var + LN_EPS)
    acts = []
    for j, cen in enumerate(cens):
        cols = slice(j * LANES, (j + 1) * LANES)
        yn = cen * inv * lg_ref[:, cols] + lb_ref[:, cols]
        acts.append((yn * jax.nn.sigmoid(yn)).astype(BF16))

    zs = []
    for c in range(SSM_BLOCKS):
        st = slice(c * BLOCK_STATES, (c + 1) * BLOCK_STATES)
        su = sut_ref[c]
        hbuf_ref[...] = jnp.dot(su.astype(BF16), wb_ref[c], preferred_element_type=F32)
        ar = jnp.broadcast_to(are_ref[:, st], (SUBLANES, BLOCK_STATES))
        ai = jnp.broadcast_to(aim_ref[:, st], (SUBLANES, BLOCK_STATES))

        def seq_body(rb, carry, st=st, ar=ar, ai=ai):
            ro = pl.multiple_of(rb * SUBLANES, SUBLANES)

            def step_body(t, h):
                hr, hi = h
                o = pl.multiple_of(t * bb + ro, SUBLANES)
                nr = ar * hr - ai * hi + hbuf_ref[pl.ds(o, SUBLANES), :BLOCK_STATES]
                ni = ar * hi + ai * hr + hbuf_ref[pl.ds(o, SUBLANES), BLOCK_STATES:]
                hbuf_ref[pl.ds(o, SUBLANES), :BLOCK_STATES] = nr
                hbuf_ref[pl.ds(o, SUBLANES), BLOCK_STATES:] = ni
                return nr, ni

            h0 = (hre_ref[pl.ds(ro, SUBLANES), st], him_ref[pl.ds(ro, SUBLANES), st])
            hr, hi = lax.fori_loop(0, steps, step_body, h0, unroll=2)
            hre_ref[pl.ds(ro, SUBLANES), st] = hr
            him_ref[pl.ds(ro, SUBLANES), st] = hi
            return carry

        lax.fori_loop(0, bb // SUBLANES, seq_body, 0)
        hb = hbuf_ref[...].astype(BF16)
        y = (jnp.dot(hb[:, :BLOCK_STATES], wcre_ref[c], preferred_element_type=F32)
             - jnp.dot(hb[:, BLOCK_STATES:], wcim_ref[c], preferred_element_type=F32))
        zs.append(jax.nn.gelu(y + d_ref[:, c * LANES:(c + 1) * LANES] * su))

    z = jnp.concatenate(zs, axis=-1)
    gate = jnp.dot(z.astype(BF16), gw_ref[...], preferred_element_type=F32) + gb_ref[...]
    outs = (z * jax.nn.sigmoid(gate)).astype(BF16)

    mix = jnp.dot(jnp.concatenate(acts + [outs], axis=-1), wo_ref[...], preferred_element_type=F32)
    for j in range(MODEL_BLOCKS):
        mix_ref[j] = mix[:, j * LANES:(j + 1) * LANES]
    for b in range(bb):
        for j in range(MODEL_BLOCKS):
            cols = slice(j * LANES, (j + 1) * LANES)
            x2_ref[b, :, cols] = x1_ref[b, :, cols] + mix_ref[j, pl.ds(b, steps, stride=bb), :]


def _mixer(uc, su, x1, cs0, h0re, h0im, consts, *, bb, steps):
    nseq, total, _ = uc.shape
    groups, chunks = nseq // bb, total // steps
    rows, tail_rows = steps * bb, CONV_TAIL * bb
    assert nseq % bb == 0 and total % steps == 0 and bb % SUBLANES == 0
    assert rows % CONV_ROW_GROUP == 0 and (chunks == 1 or rows >= tail_rows)
    seq_spec = lambda w: pl.BlockSpec((bb, steps, w), lambda g, i: (g, i, 0))
    state_spec = pl.BlockSpec((bb, N_STATE), lambda g, i: (g, 0))
    cs_spec = pl.BlockSpec((bb, CONV_TAIL, D_CONV), lambda g, i: (g, 0, 0))
    return pl.pallas_call(
        functools.partial(_mixer_kernel, bb=bb, steps=steps, chunks=chunks),
        grid=(groups, chunks),
        in_specs=[seq_spec(D_CONV), seq_spec(D_SSM), seq_spec(D_MODEL), cs_spec, state_spec, state_spec]
                 + [_const_spec(c.shape) for c in consts],
        out_specs=[seq_spec(D_MODEL), cs_spec, state_spec, state_spec],
        out_shape=[jax.ShapeDtypeStruct((nseq, total, D_MODEL), F32),
                   jax.ShapeDtypeStruct((nseq, CONV_TAIL, D_CONV), F32),
                   jax.ShapeDtypeStruct((nseq, N_STATE), F32),
                   jax.ShapeDtypeStruct((nseq, N_STATE), F32)],
        scratch_shapes=[pltpu.VMEM((CONV_BLOCKS, rows + tail_rows, LANES), F32),
                        pltpu.VMEM((SSM_BLOCKS, rows, LANES), F32),
                        pltpu.VMEM((rows, 2 * BLOCK_STATES), F32),
                        pltpu.VMEM((CONV_BLOCKS, rows, LANES), F32),
                        pltpu.VMEM((MODEL_BLOCKS, rows, LANES), F32)],
        compiler_params=pltpu.CompilerParams(dimension_semantics=("parallel", "arbitrary"),
                                             vmem_limit_bytes=VMEM_LIMIT_BYTES),
        name="mixer",
    )(uc, su, x1, cs0, h0re, h0im, *consts)


def _block_diag(blocks):
    nb, n, a, b = blocks.shape
    eye = jnp.eye(n, dtype=blocks.dtype)
    return jnp.einsum("cgab,gh->cgahb", blocks, eye).reshape(nb, n * a, n * b)


PROMPT_STEPS = 64
PROMPT_FFN_ROWS = 1024
SAMPLE_SEQS = 32
SAMPLE_FFN_ROWS = 512


def kernel(x_prompt, x_sample, state_conv, state_ssm_re, state_ssm_im, meta_tokens, ffn1_norm, ffn1_w_gate, ffn1_w_up, ffn1_w_down, mix_norm, w_in, conv_w, conv_b, conv_ln_g, conv_ln_b, ssm_lambda_re, ssm_lambda_im, ssm_log_dt, ssm_b_re, ssm_b_im, ssm_c_re, ssm_c_im, ssm_d, ssm_glu_w, ssm_glu_b, w_out, ffn2_norm, ffn2_w_gate, ffn2_w_up, ffn2_w_down, final_norm):
    assert ffn1_norm.shape[0] == 1, "one layer"
    bp, lp, _ = x_prompt.shape
    bs, ls, _ = x_sample.shape
    groups_per_block = SSM_GROUPS // SSM_BLOCKS

    row = lambda v: v.reshape(1, -1).astype(F32)
    w1 = (row(ffn1_norm[0]), ffn1_w_gate[0].astype(BF16), ffn1_w_up[0].astype(BF16),
          ffn1_w_down[0].astype(BF16), row(mix_norm[0]), w_in[0].astype(BF16))
    w2 = (row(ffn2_norm[0]), ffn2_w_gate[0].astype(BF16), ffn2_w_up[0].astype(BF16),
          ffn2_w_down[0].astype(BF16), row(final_norm))
    b_blocks = lambda b: _block_diag(
        b.reshape(SSM_BLOCKS, groups_per_block, SSM_P, SSM_GC).transpose(0, 1, 3, 2))
    c_blocks = lambda c: _block_diag(
        c.reshape(SSM_BLOCKS, groups_per_block, SSM_GC, SSM_P).transpose(0, 1, 3, 2)).astype(BF16)
    are, aim, wb = _ssm_params(row(ssm_lambda_re[0]), row(ssm_lambda_im[0]),
                               row(jnp.repeat(ssm_log_dt[0], SSM_P)),
                               b_blocks(ssm_b_re[0]), b_blocks(ssm_b_im[0]))
    consts = (conv_w[0], row(conv_b[0]), row(conv_ln_g[0]), row(conv_ln_b[0]), are, aim, wb,
              c_blocks(ssm_c_re[0]), c_blocks(ssm_c_im[0]), row(ssm_d[0]),
              ssm_glu_w[0].astype(BF16), row(ssm_glu_b[0]), w_out[0].astype(BF16))

    def ffn_in(x, tm):
        nseq, steps, _ = x.shape
        x1, uc, su = _ffn_in(x.reshape(nseq * steps, D_MODEL), *w1, tm)
        return (x1.reshape(nseq, steps, D_MODEL), uc.reshape(nseq, steps, D_CONV),
                su.reshape(nseq, steps, D_SSM))

    def ffn_out(x2, tm):
        nseq, steps, _ = x2.shape
        return _ffn_out(x2.reshape(nseq * steps, D_MODEL), *w2, tm).reshape(nseq, steps, D_MODEL)

    meta = jnp.broadcast_to(meta_tokens.astype(x_prompt.dtype)[None], (bp, N_META, D_MODEL))
    x1m, ucm, sum_ = ffn_in(meta, bp * N_META)
    _, cs_m, hre_m, him_m = _mixer(
        ucm, sum_, x1m, jnp.zeros((bp, CONV_TAIL, D_CONV), F32), jnp.zeros((bp, N_STATE), F32),
        jnp.zeros((bp, N_STATE), F32), consts, bb=bp, steps=N_META)
    x1p, ucp, sup = ffn_in(x_prompt, PROMPT_FFN_ROWS)
    x2p, cs_p, hre_p, him_p = _mixer(ucp, sup, x1p, cs_m, hre_m, him_m, consts,
                                     bb=bp, steps=PROMPT_STEPS)
    y_prompt = ffn_out(x2p, PROMPT_FFN_ROWS)

    x1s, ucs, sus = ffn_in(x_sample, SAMPLE_FFN_ROWS)
    x2s, cs_s, hre_s, him_s = _mixer(
        ucs, sus, x1s, state_conv[0], state_ssm_re[0].reshape(bs, N_STATE),
        state_ssm_im[0].reshape(bs, N_STATE), consts, bb=SAMPLE_SEQS, steps=ls)
    y_sample = ffn_out(x2s, SAMPLE_FFN_ROWS)

    state = lambda h, b: h.reshape(1, b, SSM_GROUPS, SSM_P)
    return (y_prompt, y_sample, cs_p[None], state(hre_p, bp), state(him_p, bp),
            cs_s[None], state(hre_s, bs), state(him_s, bs))
```

```python
import functools

import jax
import jax.numpy as jnp
from jax import lax
from jax.experimental import pallas as pl
from jax.experimental.pallas import tpu as pltpu

F32 = jnp.float32
BF16 = jnp.bfloat16

D_MODEL = 1024
D_FF = 2816
D_CONV = 512
D_SSM = 512
CONV_K = 31
CONV_TAIL = CONV_K - 1
N_META = 16
SSM_GROUPS = 32
SSM_GC = 16
SSM_P = 64
N_STATE = SSM_GROUPS * SSM_P
RMS_EPS = 1e-6
LN_EPS = 1e-5

SUBLANES = 8
LANES = 128
FF_CHUNK = 256
N_FF_CHUNKS = D_FF // FF_CHUNK
CONV_BLOCKS = D_CONV // LANES
SSM_BLOCKS = D_SSM // LANES
MODEL_BLOCKS = D_MODEL // LANES
BLOCK_STATES = N_STATE // SSM_BLOCKS
CONV_ROW_GROUP = 8 * SUBLANES
VMEM_LIMIT_BYTES = 60 * 1024 * 1024


def _const_spec(shape):
    zeros = (0,) * len(shape)
    return pl.BlockSpec(shape, lambda *_: zeros, pipeline_mode=pl.Buffered(1))


def _rmsnorm(x, g):
    return x * lax.rsqrt(jnp.mean(x * x, axis=-1, keepdims=True) + RMS_EPS) * g


def _swiglu(hb, wg_ref, wu_ref, wd_ref):
    acc = jnp.zeros((hb.shape[0], D_MODEL), F32)
    for c in range(N_FF_CHUNKS):
        cols = slice(c * FF_CHUNK, (c + 1) * FF_CHUNK)
        g = jnp.dot(hb, wg_ref[:, cols], preferred_element_type=F32)
        u = jnp.dot(hb, wu_ref[:, cols], preferred_element_type=F32)
        a = (g * jax.nn.sigmoid(g) * u).astype(BF16)
        acc = acc + jnp.dot(a, wd_ref[cols, :], preferred_element_type=F32)
    return acc


def _ffn_in_kernel(x_ref, n1_ref, wg_ref, wu_ref, wd_ref, n2_ref, win_ref,
                   x1_ref, uc_ref, su_ref):
    x = x_ref[...]
    hb = _rmsnorm(x, n1_ref[...]).astype(BF16)
    x1 = x + 0.5 * _swiglu(hb, wg_ref, wu_ref, wd_ref)
    x1_ref[...] = x1
    h2 = _rmsnorm(x1, n2_ref[...]).astype(BF16)
    proj = jnp.dot(h2, win_ref[...], preferred_element_type=F32)
    cv = proj[:, :D_CONV]
    cg = proj[:, D_CONV:2 * D_CONV]
    uc_ref[...] = cv * jax.nn.sigmoid(cg)
    su_ref[...] = proj[:, 2 * D_CONV:]


def _ffn_in(x, n1, wg, wu, wd, n2, win, tm):
    rows = x.shape[0]
    assert rows % tm == 0
    row_spec = lambda w: pl.BlockSpec((tm, w), lambda i: (i, 0))
    return pl.pallas_call(
        _ffn_in_kernel,
        grid=(rows // tm,),
        in_specs=[row_spec(D_MODEL), _const_spec(n1.shape), _const_spec(wg.shape),
                  _const_spec(wu.shape), _const_spec(wd.shape), _const_spec(n2.shape),
                  _const_spec(win.shape)],
        out_specs=[row_spec(D_MODEL), row_spec(D_CONV), row_spec(D_SSM)],
        out_shape=[jax.ShapeDtypeStruct((rows, D_MODEL), F32),
                   jax.ShapeDtypeStruct((rows, D_CONV), F32),
                   jax.ShapeDtypeStruct((rows, D_SSM), F32)],
        compiler_params=pltpu.CompilerParams(dimension_semantics=("parallel",),
                                             vmem_limit_bytes=VMEM_LIMIT_BYTES),
        name="ffn_in",
    )(x, n1, wg, wu, wd, n2, win)


def _ssm_param_kernel(lre_ref, lim_ref, ldt_ref, bre_ref, bim_ref, are_ref, aim_ref, wb_ref):
    lre = lre_ref[...]
    lim = lim_ref[...]
    dt = jnp.exp(ldt_ref[...])
    mag = jnp.exp(lre * dt)
    abr = mag * jnp.cos(lim * dt)
    abi = mag * jnp.sin(lim * dt)
    are_ref[...] = abr
    aim_ref[...] = abi
    nr = abr - 1.0
    den = lre * lre + lim * lim
    cr = (nr * lre + abi * lim) / den
    ci = (abi * lre - nr * lim) / den
    for c in range(SSM_BLOCKS):
        sl = slice(c * BLOCK_STATES, (c + 1) * BLOCK_STATES)
        crc, cic = cr[:, sl], ci[:, sl]
        br, bi = bre_ref[c], bim_ref[c]
        wb_ref[c, :, :BLOCK_STATES] = (crc * br - cic * bi).astype(BF16)
        wb_ref[c, :, BLOCK_STATES:] = (crc * bi + cic * br).astype(BF16)


def _ssm_params(lre, lim, ldt, bre, bim):
    return pl.pallas_call(
        _ssm_param_kernel,
        out_shape=[jax.ShapeDtypeStruct((1, N_STATE), F32),
                   jax.ShapeDtypeStruct((1, N_STATE), F32),
                   jax.ShapeDtypeStruct((SSM_BLOCKS, LANES, 2 * BLOCK_STATES), BF16)],
        name="ssm_params",
    )(lre, lim, ldt, bre, bim)


def _to_step_major(dst_ref, base, src_ref, *, bb, steps):
    for b in range(bb):
        for j in range(src_ref.shape[-1] // LANES):
            dst_ref[j, pl.ds(base + b, steps, stride=bb), :] = src_ref[b, :, j * LANES:(j + 1) * LANES]


def _conv_state_out(cs_ref, cs0_ref, uc_ref, steps):
    if steps >= CONV_TAIL:
        cs_ref[...] = uc_ref[:, steps - CONV_TAIL:, :]
    else:
        cs_ref[:, :CONV_TAIL - steps, :] = cs0_ref[:, steps:, :]
        cs_ref[:, CONV_TAIL - steps:, :] = uc_ref[...]


def _ssm_block(c, su, hbuf_ref, are_ref, aim_ref, wb_ref, hre_ref, him_ref, *, bb, steps, live):
    st = slice(c * BLOCK_STATES, (c + 1) * BLOCK_STATES)
    hbuf_ref[...] = jnp.dot(su.astype(BF16), wb_ref[c], preferred_element_type=F32)
    ar = jnp.broadcast_to(are_ref[:, st], (SUBLANES, BLOCK_STATES))
    ai = jnp.broadcast_to(aim_ref[:, st], (SUBLANES, BLOCK_STATES))
    for rb in range(bb // SUBLANES):
        seqs = slice(rb * SUBLANES, (rb + 1) * SUBLANES)
        hr0, hi0 = hre_ref[seqs, st], him_ref[seqs, st]
        hr, hi = hr0, hi0
        for t in range(steps):
            rows = slice(t * bb + rb * SUBLANES, t * bb + (rb + 1) * SUBLANES)
            nr = ar * hr - ai * hi + hbuf_ref[rows, :BLOCK_STATES]
            ni = ar * hi + ai * hr + hbuf_ref[rows, BLOCK_STATES:]
            hbuf_ref[rows, :BLOCK_STATES] = nr
            hbuf_ref[rows, BLOCK_STATES:] = ni
            hr, hi = nr, ni
        hre_ref[seqs, st] = jnp.where(live, hr, hr0)
        him_ref[seqs, st] = jnp.where(live, hi, hi0)


def _mixer_state_kernel(uc_ref, su_ref, cs0_ref, h0re_ref, h0im_ref, are_ref, aim_ref, wb_ref,
                        cs_ref, hre_ref, him_ref, sut_ref, hbuf_ref, *, bb, steps):
    _conv_state_out(cs_ref, cs0_ref, uc_ref, steps)
    hre_ref[...] = h0re_ref[...]
    him_ref[...] = h0im_ref[...]
    _to_step_major(sut_ref, 0, su_ref, bb=bb, steps=steps)
    for c in range(SSM_BLOCKS):
        _ssm_block(c, sut_ref[c], hbuf_ref, are_ref, aim_ref, wb_ref, hre_ref, him_ref,
                   bb=bb, steps=steps, live=True)


def _mixer_state(uc, su, cs0, h0re, h0im, are, aim, wb):
    bb, steps, _ = uc.shape
    rows = bb * steps
    return pl.pallas_call(
        functools.partial(_mixer_state_kernel, bb=bb, steps=steps),
        out_shape=[jax.ShapeDtypeStruct((bb, CONV_TAIL, D_CONV), F32),
                   jax.ShapeDtypeStruct((bb, N_STATE), F32),
                   jax.ShapeDtypeStruct((bb, N_STATE), F32)],
        scratch_shapes=[pltpu.VMEM((SSM_BLOCKS, rows, LANES), F32),
                        pltpu.VMEM((rows, 2 * BLOCK_STATES), F32)],
        name="mixer_state",
    )(uc, su, cs0, h0re, h0im, are, aim, wb)


def _mixer_ffn_kernel(uc_ref, su_ref, x1_ref, cs0_ref, h0re_ref, h0im_ref,
                      cw_ref, cb_ref, lg_ref, lb_ref, are_ref, aim_ref, wb_ref, wcre_ref, wcim_ref,
                      d_ref, gw_ref, gb_ref, wo_ref,
                      n2_ref, wg_ref, wu_ref, wd_ref, nf_ref,
                      y_ref, cs_ref, hre_ref, him_ref,
                      ext_ref, sut_ref, hbuf_ref, yc_ref, mix_ref, x2_ref, *, bb, steps, chunks, n):
    rows = steps * bb
    tail_rows = CONV_TAIL * bb
    j = pl.program_id(0)
    live = j < n
    chunk = lax.rem(j, chunks)
    slot = lax.rem(j, 2)

    @pl.when(j == 0)
    def _():
        x2_ref[...] = jnp.zeros_like(x2_ref)

    @pl.when(jnp.logical_and(live, chunk == 0))
    def _():
        hre_ref[...] = h0re_ref[...]
        him_ref[...] = h0im_ref[...]
        _to_step_major(ext_ref, 0, cs0_ref, bb=bb, steps=CONV_TAIL)

    @pl.when(jnp.logical_and(live, chunk == chunks - 1))
    def _():
        _conv_state_out(cs_ref, cs0_ref, uc_ref, steps)

    x2 = x2_ref[1 - slot]
    hb = _rmsnorm(x2, n2_ref[...]).astype(BF16)
    x3 = x2 + 0.5 * _swiglu(hb, wg_ref, wu_ref, wd_ref)
    y_ref[...] = _rmsnorm(x3, nf_ref[...]).reshape(bb, steps, D_MODEL)

    _to_step_major(ext_ref, tail_rows, uc_ref, bb=bb, steps=steps)
    _to_step_major(sut_ref, 0, su_ref, bb=bb, steps=steps)

    for r in range(rows // CONV_ROW_GROUP):
        o = r * CONV_ROW_GROUP
        for k in range(CONV_BLOCKS):
            cols = slice(k * LANES, (k + 1) * LANES)
            accs = [jnp.zeros((SUBLANES, LANES), F32)] * (CONV_ROW_GROUP // SUBLANES)
            for tap in range(CONV_K):
                wt = cw_ref[tap:tap + 1, cols]
                accs = [a + wt * ext_ref[k, o + q * SUBLANES + tap * bb:o + (q + 1) * SUBLANES + tap * bb, :]
                        for q, a in enumerate(accs)]
            for q, a in enumerate(accs):
                yc_ref[k, o + q * SUBLANES:o + (q + 1) * SUBLANES, :] = a + cb_ref[:, cols]

    if chunks > 1:
        ext_ref[:, 0:tail_rows, :] = ext_ref[:, rows:rows + tail_rows, :]

    ycs = [yc_ref[k] for k in range(CONV_BLOCKS)]
    mu = sum(jnp.sum(y, axis=-1, keepdims=True) for y in ycs) * (1.0 / D_CONV)
    cens = [y - mu for y in ycs]
    var = sum(jnp.sum(c * c, axis=-1, keepdims=True) for c in cens) * (1.0 / D_CONV)
    inv = lax.rsqrt(var + LN_EPS)
    acts = []
    for k, cen in enumerate(cens):
        cols = slice(k * LANES, (k + 1) * LANES)
        yn = cen * inv * lg_ref[:, cols] + lb_ref[:, cols]
        acts.append((yn * jax.nn.sigmoid(yn)).astype(BF16))

    zs = []
    for c in range(SSM_BLOCKS):
        su = sut_ref[c]
        _ssm_block(c, su, hbuf_ref, are_ref, aim_ref, wb_ref, hre_ref, him_ref,
                   bb=bb, steps=steps, live=live)
        hb = hbuf_ref[...].astype(BF16)
        y = (jnp.dot(hb[:, :BLOCK_STATES], wcre_ref[c], preferred_element_type=F32)
             - jnp.dot(hb[:, BLOCK_STATES:], wcim_ref[c], preferred_element_type=F32))
        zs.append(jax.nn.gelu(y + d_ref[:, c * LANES:(c + 1) * LANES] * su))

    z = jnp.concatenate(zs, axis=-1)
    gate = jnp.dot(z.astype(BF16), gw_ref[...], preferred_element_type=F32) + gb_ref[...]
    outs = (z * jax.nn.sigmoid(gate)).astype(BF16)

    mix = jnp.dot(jnp.concatenate(acts + [outs], axis=-1), wo_ref[...], preferred_element_type=F32)
    for k in range(MODEL_BLOCKS):
        mix_ref[k] = mix[:, k * LANES:(k + 1) * LANES]
    for b in range(bb):
        for k in range(MODEL_BLOCKS):
            cols = slice(k * LANES, (k + 1) * LANES)
            x2_ref[slot, b * steps:(b + 1) * steps, cols] = (
                x1_ref[b, :, cols] + mix_ref[k, pl.ds(b, steps, stride=bb), :])


def _mixer_ffn(uc, su, x1, cs0, h0re, h0im, consts, ffn, *, bb, steps):
    nseq, total, _ = uc.shape
    groups, chunks = nseq // bb, total // steps
    n = groups * chunks
    rows, tail_rows = steps * bb, CONV_TAIL * bb
    assert nseq % bb == 0 and total % steps == 0 and bb % SUBLANES == 0 and steps % SUBLANES == 0
    assert rows % CONV_ROW_GROUP == 0 and (chunks == 1 or rows >= tail_rows)
    cur = lambda j: jnp.minimum(j, n - 1)
    prev = lambda j: jnp.maximum(j - 1, 0)
    group = lambda c: lax.div(c, chunks)
    part = lambda c: lax.rem(c, chunks)
    seq_spec = lambda w: pl.BlockSpec((bb, steps, w), lambda j: (group(cur(j)), part(cur(j)), 0))
    state_spec = pl.BlockSpec((bb, N_STATE), lambda j: (group(cur(j)), 0))
    cs_spec = pl.BlockSpec((bb, CONV_TAIL, D_CONV), lambda j: (group(cur(j)), 0, 0))
    y_spec = pl.BlockSpec((bb, steps, D_MODEL), lambda j: (group(prev(j)), part(prev(j)), 0))
    return pl.pallas_call(
        functools.partial(_mixer_ffn_kernel, bb=bb, steps=steps, chunks=chunks, n=n),
        grid=(n + 1,),
        in_specs=[seq_spec(D_CONV), seq_spec(D_SSM), seq_spec(D_MODEL), cs_spec, state_spec, state_spec]
                 + [_const_spec(c.shape) for c in consts + ffn],
        out_specs=[y_spec, cs_spec, state_spec, state_spec],
        out_shape=[jax.ShapeDtypeStruct((nseq, total, D_MODEL), F32),
                   jax.ShapeDtypeStruct((nseq, CONV_TAIL, D_CONV), F32),
                   jax.ShapeDtypeStruct((nseq, N_STATE), F32),
                   jax.ShapeDtypeStruct((nseq, N_STATE), F32)],
        scratch_shapes=[pltpu.VMEM((CONV_BLOCKS, rows + tail_rows, LANES), F32),
                        pltpu.VMEM((SSM_BLOCKS, rows, LANES), F32),
                        pltpu.VMEM((rows, 2 * BLOCK_STATES), F32),
                        pltpu.VMEM((CONV_BLOCKS, rows, LANES), F32),
                        pltpu.VMEM((MODEL_BLOCKS, rows, LANES), F32),
                        pltpu.VMEM((2, rows, D_MODEL), F32)],
        compiler_params=pltpu.CompilerParams(dimension_semantics=("arbitrary",),
                                             vmem_limit_bytes=VMEM_LIMIT_BYTES),
        name="mixer_ffn",
    )(uc, su, x1, cs0, h0re, h0im, *consts, *ffn)


def _block_diag(blocks):
    nb, n, a, b = blocks.shape
    eye = jnp.eye(n, dtype=blocks.dtype)
    return jnp.einsum("cgab,gh->cgahb", blocks, eye).reshape(nb, n * a, n * b)


PROMPT_STEPS = 64
PROMPT_FFN_ROWS = 1024
SAMPLE_SEQS = 32
SAMPLE_FFN_ROWS = 512


def kernel(x_prompt, x_sample, state_conv, state_ssm_re, state_ssm_im, meta_tokens, ffn1_norm, ffn1_w_gate, ffn1_w_up, ffn1_w_down, mix_norm, w_in, conv_w, conv_b, conv_ln_g, conv_ln_b, ssm_lambda_re, ssm_lambda_im, ssm_log_dt, ssm_b_re, ssm_b_im, ssm_c_re, ssm_c_im, ssm_d, ssm_glu_w, ssm_glu_b, w_out, ffn2_norm, ffn2_w_gate, ffn2_w_up, ffn2_w_down, final_norm):
    assert ffn1_norm.shape[0] == 1, "one layer"
    bp, lp, _ = x_prompt.shape
    bs, ls, _ = x_sample.shape
    groups_per_block = SSM_GROUPS // SSM_BLOCKS

    row = lambda v: v.reshape(1, -1).astype(F32)
    w1 = (row(ffn1_norm[0]), ffn1_w_gate[0].astype(BF16), ffn1_w_up[0].astype(BF16),
          ffn1_w_down[0].astype(BF16), row(mix_norm[0]), w_in[0].astype(BF16))
    w2 = (row(ffn2_norm[0]), ffn2_w_gate[0].astype(BF16), ffn2_w_up[0].astype(BF16),
          ffn2_w_down[0].astype(BF16), row(final_norm))
    b_blocks = lambda b: _block_diag(
        b.reshape(SSM_BLOCKS, groups_per_block, SSM_P, SSM_GC).transpose(0, 1, 3, 2))
    c_blocks = lambda c: _block_diag(
        c.reshape(SSM_BLOCKS, groups_per_block, SSM_GC, SSM_P).transpose(0, 1, 3, 2)).astype(BF16)
    are, aim, wb = _ssm_params(row(ssm_lambda_re[0]), row(ssm_lambda_im[0]),
                               row(jnp.repeat(ssm_log_dt[0], SSM_P)),
                               b_blocks(ssm_b_re[0]), b_blocks(ssm_b_im[0]))
    consts = (conv_w[0], row(conv_b[0]), row(conv_ln_g[0]), row(conv_ln_b[0]), are, aim, wb,
              c_blocks(ssm_c_re[0]), c_blocks(ssm_c_im[0]), row(ssm_d[0]),
              ssm_glu_w[0].astype(BF16), row(ssm_glu_b[0]), w_out[0].astype(BF16))

    def ffn_in(x, tm):
        nseq, steps, _ = x.shape
        x1, uc, su = _ffn_in(x.reshape(nseq * steps, D_MODEL), *w1, tm)
        return (x1.reshape(nseq, steps, D_MODEL), uc.reshape(nseq, steps, D_CONV),
                su.reshape(nseq, steps, D_SSM))

    meta = jnp.broadcast_to(meta_tokens.astype(x_prompt.dtype)[None], (bp, N_META, D_MODEL))
    _, ucm, sum_ = ffn_in(meta, bp * N_META)
    cs_m, hre_m, him_m = _mixer_state(
        ucm, sum_, jnp.zeros((bp, CONV_TAIL, D_CONV), F32), jnp.zeros((bp, N_STATE), F32),
        jnp.zeros((bp, N_STATE), F32), are, aim, wb)
    x1p, ucp, sup = ffn_in(x_prompt, PROMPT_FFN_ROWS)
    y_prompt, cs_p, hre_p, him_p = _mixer_ffn(ucp, sup, x1p, cs_m, hre_m, him_m, consts, w2,
                                              bb=bp, steps=PROMPT_STEPS)

    x1s, ucs, sus = ffn_in(x_sample, SAMPLE_FFN_ROWS)
    y_sample, cs_s, hre_s, him_s = _mixer_ffn(
        ucs, sus, x1s, state_conv[0], state_ssm_re[0].reshape(bs, N_STATE),
        state_ssm_im[0].reshape(bs, N_STATE), consts, w2, bb=SAMPLE_SEQS, steps=ls)

    state = lambda h, b: h.reshape(1, b, SSM_GROUPS, SSM_P)
    return (y_prompt, y_sample, cs_p[None], state(hre_p, bp), state(him_p, bp),
            cs_s[None], state(hre_s, bs), state(him_s, bs))
```

```python
import functools

import jax
import jax.numpy as jnp
from jax import lax
from jax.experimental import pallas as pl
from jax.experimental.pallas import tpu as pltpu

F32 = jnp.float32
BF16 = jnp.bfloat16

D_MODEL = 1024
D_FF = 2816
D_CONV = 512
D_SSM = 512
CONV_K = 31
CONV_TAIL = CONV_K - 1
N_META = 16
SSM_GROUPS = 32
SSM_GC = 16
SSM_P = 64
N_STATE = SSM_GROUPS * SSM_P
RMS_EPS = 1e-6
LN_EPS = 1e-5

SUBLANES = 8
LANES = 128
FF_CHUNK = 256
N_FF_CHUNKS = D_FF // FF_CHUNK
CONV_BLOCKS = D_CONV // LANES
SSM_BLOCKS = D_SSM // LANES
MODEL_BLOCKS = D_MODEL // LANES
BLOCK_STATES = N_STATE // SSM_BLOCKS
CONV_ROW_GROUP = 8 * SUBLANES
VMEM_LIMIT_BYTES = 60 * 1024 * 1024


def _const_spec(shape):
    zeros = (0,) * len(shape)
    return pl.BlockSpec(shape, lambda *_: zeros, pipeline_mode=pl.Buffered(1))


def _rmsnorm(x, g):
    return x * lax.rsqrt(jnp.mean(x * x, axis=-1, keepdims=True) + RMS_EPS) * g


def _swiglu(hb, wg_ref, wu_ref, wd_ref):
    acc = jnp.zeros((hb.shape[0], D_MODEL), F32)
    for c in range(N_FF_CHUNKS):
        cols = slice(c * FF_CHUNK, (c + 1) * FF_CHUNK)
        g = jnp.dot(hb, wg_ref[:, cols], preferred_element_type=F32)
        u = jnp.dot(hb, wu_ref[:, cols], preferred_element_type=F32)
        a = (g * jax.nn.sigmoid(g) * u).astype(BF16)
        acc = acc + jnp.dot(a, wd_ref[cols, :], preferred_element_type=F32)
    return acc


def _ffn_in_rows(x_ref, n1_ref, wg_ref, wu_ref, wd_ref, n2_ref, win_ref, x1_ref, uc_ref, su_ref):
    x = x_ref[...]
    hb = _rmsnorm(x, n1_ref[...]).astype(BF16)
    x1 = x + 0.5 * _swiglu(hb, wg_ref, wu_ref, wd_ref)
    if x1_ref is not None:
        x1_ref[...] = x1
    h2 = _rmsnorm(x1, n2_ref[...]).astype(BF16)
    proj = jnp.dot(h2, win_ref[...], preferred_element_type=F32)
    cv = proj[:, :D_CONV]
    cg = proj[:, D_CONV:2 * D_CONV]
    uc_ref[...] = cv * jax.nn.sigmoid(cg)
    su_ref[...] = proj[:, 2 * D_CONV:]


def _ffn_in_kernel(x_ref, *refs):
    _ffn_in_rows(x_ref, *refs)


def _ffn_in(x, weights, tm):
    rows = x.shape[0]
    assert rows % tm == 0
    row_spec = lambda w: pl.BlockSpec((tm, w), lambda i: (i, 0))
    return pl.pallas_call(
        _ffn_in_kernel,
        grid=(rows // tm,),
        in_specs=[row_spec(D_MODEL)] + [_const_spec(w.shape) for w in weights],
        out_specs=[row_spec(D_MODEL), row_spec(D_CONV), row_spec(D_SSM)],
        out_shape=[jax.ShapeDtypeStruct((rows, D_MODEL), F32),
                   jax.ShapeDtypeStruct((rows, D_CONV), F32),
                   jax.ShapeDtypeStruct((rows, D_SSM), F32)],
        compiler_params=pltpu.CompilerParams(dimension_semantics=("parallel",),
                                             vmem_limit_bytes=VMEM_LIMIT_BYTES),
        name="ffn_in",
    )(x, *weights)


def _ffn_in_prefixed_kernel(x_ref, pre_ref, n1_ref, wg_ref, wu_ref, wd_ref, n2_ref, win_ref,
                            x1_ref, uc_ref, su_ref, ucp_ref, sup_ref, *, tiles):
    weights = (n1_ref, wg_ref, wu_ref, wd_ref, n2_ref, win_ref)
    i = pl.program_id(0)

    @pl.when(i < tiles)
    def _():
        _ffn_in_rows(x_ref, *weights, x1_ref, uc_ref, su_ref)

    @pl.when(i == tiles)
    def _():
        _ffn_in_rows(pre_ref, *weights, None, ucp_ref, sup_ref)


def _ffn_in_prefixed(x, prefix, weights, tm):
    rows, pre_rows = x.shape[0], prefix.shape[0]
    assert rows % tm == 0
    tiles = rows // tm
    row_spec = lambda w: pl.BlockSpec((tm, w), lambda i: (jnp.minimum(i, tiles - 1), 0))
    pre_spec = lambda w: pl.BlockSpec((pre_rows, w), lambda i: (0, 0))
    return pl.pallas_call(
        functools.partial(_ffn_in_prefixed_kernel, tiles=tiles),
        grid=(tiles + 1,),
        in_specs=[row_spec(D_MODEL), pre_spec(D_MODEL)] + [_const_spec(w.shape) for w in weights],
        out_specs=[row_spec(D_MODEL), row_spec(D_CONV), row_spec(D_SSM),
                   pre_spec(D_CONV), pre_spec(D_SSM)],
        out_shape=[jax.ShapeDtypeStruct((rows, D_MODEL), F32),
                   jax.ShapeDtypeStruct((rows, D_CONV), F32),
                   jax.ShapeDtypeStruct((rows, D_SSM), F32),
                   jax.ShapeDtypeStruct((pre_rows, D_CONV), F32),
                   jax.ShapeDtypeStruct((pre_rows, D_SSM), F32)],
        compiler_params=pltpu.CompilerParams(dimension_semantics=("arbitrary",),
                                             vmem_limit_bytes=VMEM_LIMIT_BYTES),
        name="ffn_in_prefixed",
    )(x, prefix, *weights)


def _ssm_param_kernel(lre_ref, lim_ref, ldt_ref, bre_ref, bim_ref, are_ref, aim_ref, wb_ref):
    lre = lre_ref[...]
    lim = lim_ref[...]
    dt = jnp.exp(ldt_ref[...])
    mag = jnp.exp(lre * dt)
    abr = mag * jnp.cos(lim * dt)
    abi = mag * jnp.sin(lim * dt)
    are_ref[...] = abr
    aim_ref[...] = abi
    nr = abr - 1.0
    den = lre * lre + lim * lim
    cr = (nr * lre + abi * lim) / den
    ci = (abi * lre - nr * lim) / den
    for c in range(SSM_BLOCKS):
        sl = slice(c * BLOCK_STATES, (c + 1) * BLOCK_STATES)
        crc, cic = cr[:, sl], ci[:, sl]
        br, bi = bre_ref[c], bim_ref[c]
        wb_ref[c, :, :BLOCK_STATES] = (crc * br - cic * bi).astype(BF16)
        wb_ref[c, :, BLOCK_STATES:] = (crc * bi + cic * br).astype(BF16)


def _ssm_params(lre, lim, ldt, bre, bim):
    return pl.pallas_call(
        _ssm_param_kernel,
        out_shape=[jax.ShapeDtypeStruct((1, N_STATE), F32),
                   jax.ShapeDtypeStruct((1, N_STATE), F32),
                   jax.ShapeDtypeStruct((SSM_BLOCKS, LANES, 2 * BLOCK_STATES), BF16)],
        name="ssm_params",
    )(lre, lim, ldt, bre, bim)


def _to_step_major(dst_ref, base, src_ref, *, bb, steps):
    for b in range(bb):
        for j in range(src_ref.shape[-1] // LANES):
            dst_ref[j, pl.ds(base + b, steps, stride=bb), :] = src_ref[b, :, j * LANES:(j + 1) * LANES]


def _conv_state_out(cs_ref, cs0_ref, uc_ref, steps):
    if steps >= CONV_TAIL:
        cs_ref[...] = uc_ref[:, steps - CONV_TAIL:, :]
    else:
        cs_ref[:, :CONV_TAIL - steps, :] = cs0_ref[:, steps:, :]
        cs_ref[:, CONV_TAIL - steps:, :] = uc_ref[...]


def _ssm_block(c, su, hbuf_ref, are_ref, aim_ref, wb_ref, hre_ref, him_ref, *, bb, steps):
    st = slice(c * BLOCK_STATES, (c + 1) * BLOCK_STATES)
    hbuf_ref[...] = jnp.dot(su.astype(BF16), wb_ref[c], preferred_element_type=F32)
    ar = jnp.broadcast_to(are_ref[:, st], (SUBLANES, BLOCK_STATES))
    ai = jnp.broadcast_to(aim_ref[:, st], (SUBLANES, BLOCK_STATES))
    for rb in range(bb // SUBLANES):
        seqs = slice(rb * SUBLANES, (rb + 1) * SUBLANES)
        hr, hi = hre_ref[seqs, st], him_ref[seqs, st]
        for t in range(steps):
            rows = slice(t * bb + rb * SUBLANES, t * bb + (rb + 1) * SUBLANES)
            nr = ar * hr - ai * hi + hbuf_ref[rows, :BLOCK_STATES]
            ni = ar * hi + ai * hr + hbuf_ref[rows, BLOCK_STATES:]
            hbuf_ref[rows, :BLOCK_STATES] = nr
            hbuf_ref[rows, BLOCK_STATES:] = ni
            hr, hi = nr, ni
        hre_ref[seqs, st] = hr
        him_ref[seqs, st] = hi


def _mixer_state_kernel(uc_ref, su_ref, cs0_ref, h0re_ref, h0im_ref, are_ref, aim_ref, wb_ref,
                        cs_ref, hre_ref, him_ref, sut_ref, hbuf_ref, *, bb, steps):
    _conv_state_out(cs_ref, cs0_ref, uc_ref, steps)
    hre_ref[...] = h0re_ref[...]
    him_ref[...] = h0im_ref[...]
    _to_step_major(sut_ref, 0, su_ref, bb=bb, steps=steps)
    for c in range(SSM_BLOCKS):
        _ssm_block(c, sut_ref[c], hbuf_ref, are_ref, aim_ref, wb_ref, hre_ref, him_ref,
                   bb=bb, steps=steps)


def _mixer_state(uc, su, cs0, h0re, h0im, are, aim, wb):
    bb, steps, _ = uc.shape
    rows = bb * steps
    return pl.pallas_call(
        functools.partial(_mixer_state_kernel, bb=bb, steps=steps),
        out_shape=[jax.ShapeDtypeStruct((bb, CONV_TAIL, D_CONV), F32),
                   jax.ShapeDtypeStruct((bb, N_STATE), F32),
                   jax.ShapeDtypeStruct((bb, N_STATE), F32)],
        scratch_shapes=[pltpu.VMEM((SSM_BLOCKS, rows, LANES), F32),
                        pltpu.VMEM((rows, 2 * BLOCK_STATES), F32)],
        name="mixer_state",
    )(uc, su, cs0, h0re, h0im, are, aim, wb)


def _mixer_ffn_kernel(uc_ref, su_ref, x1_ref, cs0_ref, h0re_ref, h0im_ref,
                      cw_ref, cb_ref, lg_ref, lb_ref, are_ref, aim_ref, wb_ref, wcre_ref, wcim_ref,
                      d_ref, gw_ref, gb_ref, wo_ref,
                      n2_ref, wg_ref, wu_ref, wd_ref, nf_ref,
                      y_ref, cs_ref, hre_ref, him_ref,
                      ext_ref, sut_ref, hbuf_ref, yc_ref, mix_ref, x2_ref, *, bb, steps, chunks):
    rows = steps * bb
    tail_rows = CONV_TAIL * bb
    chunk = pl.program_id(1)

    def ffn():
        x2 = x2_ref[...]
        hb = _rmsnorm(x2, n2_ref[...]).astype(BF16)
        x3 = x2 + 0.5 * _swiglu(hb, wg_ref, wu_ref, wd_ref)
        y_ref[...] = _rmsnorm(x3, nf_ref[...]).reshape(bb, steps, D_MODEL)

    def mixer():
        @pl.when(chunk == 0)
        def _():
            hre_ref[...] = h0re_ref[...]
            him_ref[...] = h0im_ref[...]
            _to_step_major(ext_ref, 0, cs0_ref, bb=bb, steps=CONV_TAIL)

        @pl.when(chunk == chunks - 1)
        def _():
            _conv_state_out(cs_ref, cs0_ref, uc_ref, steps)

        _to_step_major(ext_ref, tail_rows, uc_ref, bb=bb, steps=steps)
        _to_step_major(sut_ref, 0, su_ref, bb=bb, steps=steps)

        for r in range(rows // CONV_ROW_GROUP):
            o = r * CONV_ROW_GROUP
            for k in range(CONV_BLOCKS):
                cols = slice(k * LANES, (k + 1) * LANES)
                accs = [jnp.zeros((SUBLANES, LANES), F32)] * (CONV_ROW_GROUP // SUBLANES)
                for tap in range(CONV_K):
                    wt = cw_ref[tap:tap + 1, cols]
                    accs = [a + wt * ext_ref[k, o + q * SUBLANES + tap * bb:o + (q + 1) * SUBLANES + tap * bb, :]
                            for q, a in enumerate(accs)]
                for q, a in enumerate(accs):
                    yc_ref[k, o + q * SUBLANES:o + (q + 1) * SUBLANES, :] = a + cb_ref[:, cols]

        if chunks > 1:
            ext_ref[:, 0:tail_rows, :] = ext_ref[:, rows:rows + tail_rows, :]

        ycs = [yc_ref[k] for k in range(CONV_BLOCKS)]
        mu = sum(jnp.sum(y, axis=-1, keepdims=True) for y in ycs) * (1.0 / D_CONV)
        cens = [y - mu for y in ycs]
        var = sum(jnp.sum(c * c, axis=-1, keepdims=True) for c in cens) * (1.0 / D_CONV)
        inv = lax.rsqrt(var + LN_EPS)
        acts = []
        for k, cen in enumerate(cens):
            cols = slice(k * LANES, (k + 1) * LANES)
            yn = cen * inv * lg_ref[:, cols] + lb_ref[:, cols]
            acts.append((yn * jax.nn.sigmoid(yn)).astype(BF16))

        zs = []
        for c in range(SSM_BLOCKS):
            su = sut_ref[c]
            hbuf = hbuf_ref.at[c % 2]
            _ssm_block(c, su, hbuf, are_ref, aim_ref, wb_ref, hre_ref, him_ref, bb=bb, steps=steps)
            hb = hbuf[...].astype(BF16)
            y = (jnp.dot(hb[:, :BLOCK_STATES], wcre_ref[c], preferred_element_type=F32)
                 - jnp.dot(hb[:, BLOCK_STATES:], wcim_ref[c], preferred_element_type=F32))
            zs.append(jax.nn.gelu(y + d_ref[:, c * LANES:(c + 1) * LANES] * su))

        z = jnp.concatenate(zs, axis=-1)
        gate = jnp.dot(z.astype(BF16), gw_ref[...], preferred_element_type=F32) + gb_ref[...]
        outs = (z * jax.nn.sigmoid(gate)).astype(BF16)

        mix = jnp.dot(jnp.concatenate(acts + [outs], axis=-1), wo_ref[...], preferred_element_type=F32)
        for k in range(MODEL_BLOCKS):
            mix_ref[k] = mix[:, k * LANES:(k + 1) * LANES]
        for b in range(bb):
            for k in range(MODEL_BLOCKS):
                cols = slice(k * LANES, (k + 1) * LANES)
                x2_ref[b * steps:(b + 1) * steps, cols] = (
                    x1_ref[b, :, cols] + mix_ref[k, pl.ds(b, steps, stride=bb), :])

    mixer()
    ffn()


def _mixer_ffn(uc, su, x1, cs0, h0re, h0im, consts, ffn, *, bb, steps):
    nseq, total, _ = uc.shape
    groups, chunks = nseq // bb, total // steps
    rows, tail_rows = steps * bb, CONV_TAIL * bb
    assert nseq % bb == 0 and total % steps == 0 and bb % SUBLANES == 0 and steps % SUBLANES == 0
    assert rows % CONV_ROW_GROUP == 0 and (chunks == 1 or rows >= tail_rows)
    seq_spec = lambda w: pl.BlockSpec((bb, steps, w), lambda g, i: (g, i, 0))
    state_spec = pl.BlockSpec((bb, N_STATE), lambda g, i: (g, 0))
    cs_spec = pl.BlockSpec((bb, CONV_TAIL, D_CONV), lambda g, i: (g, 0, 0))
    return pl.pallas_call(
        functools.partial(_mixer_ffn_kernel, bb=bb, steps=steps, chunks=chunks),
        grid=(groups, chunks),
        in_specs=[seq_spec(D_CONV), seq_spec(D_SSM), seq_spec(D_MODEL), cs_spec, state_spec, state_spec]
                 + [_const_spec(c.shape) for c in consts + ffn],
        out_specs=[seq_spec(D_MODEL), cs_spec, state_spec, state_spec],
        out_shape=[jax.ShapeDtypeStruct((nseq, total, D_MODEL), F32),
                   jax.ShapeDtypeStruct((nseq, CONV_TAIL, D_CONV), F32),
                   jax.ShapeDtypeStruct((nseq, N_STATE), F32),
                   jax.ShapeDtypeStruct((nseq, N_STATE), F32)],
        scratch_shapes=[pltpu.VMEM((CONV_BLOCKS, rows + tail_rows, LANES), F32),
                        pltpu.VMEM((SSM_BLOCKS, rows, LANES), F32),
                        pltpu.VMEM((2, rows, 2 * BLOCK_STATES), F32),
                        pltpu.VMEM((CONV_BLOCKS, rows, LANES), F32),
                        pltpu.VMEM((MODEL_BLOCKS, rows, LANES), F32),
                        pltpu.VMEM((rows, D_MODEL), F32)],
        compiler_params=pltpu.CompilerParams(dimension_semantics=("parallel", "arbitrary"),
                                             vmem_limit_bytes=VMEM_LIMIT_BYTES),
        name="mixer_ffn",
    )(uc, su, x1, cs0, h0re, h0im, *consts, *ffn)


def _block_diag(blocks):
    nb, n, a, b = blocks.shape
    eye = jnp.eye(n, dtype=blocks.dtype)
    return jnp.einsum("cgab,gh->cgahb", blocks, eye).reshape(nb, n * a, n * b)


PROMPT_STEPS = 64
PROMPT_FFN_ROWS = 1024
SAMPLE_SEQS = 32
SAMPLE_FFN_ROWS = 512


def kernel(x_prompt, x_sample, state_conv, state_ssm_re, state_ssm_im, meta_tokens, ffn1_norm, ffn1_w_gate, ffn1_w_up, ffn1_w_down, mix_norm, w_in, conv_w, conv_b, conv_ln_g, conv_ln_b, ssm_lambda_re, ssm_lambda_im, ssm_log_dt, ssm_b_re, ssm_b_im, ssm_c_re, ssm_c_im, ssm_d, ssm_glu_w, ssm_glu_b, w_out, ffn2_norm, ffn2_w_gate, ffn2_w_up, ffn2_w_down, final_norm):
    assert ffn1_norm.shape[0] == 1, "one layer"
    bp, lp, _ = x_prompt.shape
    bs, ls, _ = x_sample.shape
    groups_per_block = SSM_GROUPS // SSM_BLOCKS

    row = lambda v: v.reshape(1, -1).astype(F32)
    w1 = (row(ffn1_norm[0]), ffn1_w_gate[0].astype(BF16), ffn1_w_up[0].astype(BF16),
          ffn1_w_down[0].astype(BF16), row(mix_norm[0]), w_in[0].astype(BF16))
    w2 = (row(ffn2_norm[0]), ffn2_w_gate[0].astype(BF16), ffn2_w_up[0].astype(BF16),
          ffn2_w_down[0].astype(BF16), row(final_norm))
    b_blocks = lambda b: _block_diag(
        b.reshape(SSM_BLOCKS, groups_per_block, SSM_P, SSM_GC).transpose(0, 1, 3, 2))
    c_blocks = lambda c: _block_diag(
        c.reshape(SSM_BLOCKS, groups_per_block, SSM_GC, SSM_P).transpose(0, 1, 3, 2)).astype(BF16)
    are, aim, wb = _ssm_params(row(ssm_lambda_re[0]), row(ssm_lambda_im[0]),
                               row(jnp.repeat(ssm_log_dt[0], SSM_P)),
                               b_blocks(ssm_b_re[0]), b_blocks(ssm_b_im[0]))
    consts = (conv_w[0], row(conv_b[0]), row(conv_ln_g[0]), row(conv_ln_b[0]), are, aim, wb,
              c_blocks(ssm_c_re[0]), c_blocks(ssm_c_im[0]), row(ssm_d[0]),
              ssm_glu_w[0].astype(BF16), row(ssm_glu_b[0]), w_out[0].astype(BF16))

    meta = jnp.broadcast_to(meta_tokens.astype(x_prompt.dtype)[None], (bp, N_META, D_MODEL))
    x1p, ucp, sup, ucm, sum_ = _ffn_in_prefixed(
        x_prompt.reshape(bp * lp, D_MODEL), meta.reshape(bp * N_META, D_MODEL), w1, PROMPT_FFN_ROWS)
    cs_m, hre_m, him_m = _mixer_state(
        ucm.reshape(bp, N_META, D_CONV), sum_.reshape(bp, N_META, D_SSM),
        jnp.zeros((bp, CONV_TAIL, D_CONV), F32), jnp.zeros((bp, N_STATE), F32),
        jnp.zeros((bp, N_STATE), F32), are, aim, wb)
    y_prompt, cs_p, hre_p, him_p = _mixer_ffn(
        ucp.reshape(bp, lp, D_CONV), sup.reshape(bp, lp, D_SSM), x1p.reshape(bp, lp, D_MODEL),
        cs_m, hre_m, him_m, consts, w2, bb=bp, steps=PROMPT_STEPS)

    x1s, ucs, sus = _ffn_in(x_sample.reshape(bs * ls, D_MODEL), w1, SAMPLE_FFN_ROWS)
    y_sample, cs_s, hre_s, him_s = _mixer_ffn(
        ucs.reshape(bs, ls, D_CONV), sus.reshape(bs, ls, D_SSM), x1s.reshape(bs, ls, D_MODEL),
        state_conv[0], state_ssm_re[0].reshape(bs, N_STATE), state_ssm_im[0].reshape(bs, N_STATE),
        consts, w2, bb=SAMPLE_SEQS, steps=ls)

    state = lambda h, b: h.reshape(1, b, SSM_GROUPS, SSM_P)
    return (y_prompt, y_sample, cs_p[None], state(hre_p, bp), state(him_p, bp),
            cs_s[None], state(hre_s, bs), state(him_s, bs))
```

```python
import functools

import jax
import jax.numpy as jnp
from jax import lax
from jax.experimental import pallas as pl
from jax.experimental.pallas import tpu as pltpu

F32 = jnp.float32
BF16 = jnp.bfloat16

D_MODEL = 1024
D_FF = 2816
D_CONV = 512
D_SSM = 512
CONV_K = 31
CONV_TAIL = CONV_K - 1
N_META = 16
SSM_GROUPS = 32
SSM_GC = 16
SSM_P = 64
N_STATE = SSM_GROUPS * SSM_P
RMS_EPS = 1e-6
LN_EPS = 1e-5

SUBLANES = 8
LANES = 128
FF_CHUNK = 256
N_FF_CHUNKS = D_FF // FF_CHUNK
CONV_BLOCKS = D_CONV // LANES
SSM_BLOCKS = D_SSM // LANES
MODEL_BLOCKS = D_MODEL // LANES
BLOCK_STATES = N_STATE // SSM_BLOCKS
CONV_ROW_GROUP = 8 * SUBLANES
VMEM_LIMIT_BYTES = 60 * 1024 * 1024


def _const_spec(shape):
    zeros = (0,) * len(shape)
    return pl.BlockSpec(shape, lambda *_: zeros, pipeline_mode=pl.Buffered(1))


def _rmsnorm(x, g):
    return x * lax.rsqrt(jnp.mean(x * x, axis=-1, keepdims=True) + RMS_EPS) * g


def _swiglu(hb, wg_ref, wu_ref, wd_ref):
    acc = jnp.zeros((hb.shape[0], D_MODEL), F32)
    for c in range(N_FF_CHUNKS):
        cols = slice(c * FF_CHUNK, (c + 1) * FF_CHUNK)
        g = jnp.dot(hb, wg_ref[:, cols], preferred_element_type=F32)
        u = jnp.dot(hb, wu_ref[:, cols], preferred_element_type=F32)
        a = (g * jax.nn.sigmoid(g) * u).astype(BF16)
        acc = acc + jnp.dot(a, wd_ref[cols, :], preferred_element_type=F32)
    return acc


def _ffn_in_rows(x_ref, n1_ref, wg_ref, wu_ref, wd_ref, n2_ref, win_ref, x1_ref, uc_ref, su_ref):
    x = x_ref[...]
    hb = _rmsnorm(x, n1_ref[...]).astype(BF16)
    x1 = x + 0.5 * _swiglu(hb, wg_ref, wu_ref, wd_ref)
    if x1_ref is not None:
        x1_ref[...] = x1
    h2 = _rmsnorm(x1, n2_ref[...]).astype(BF16)
    proj = jnp.dot(h2, win_ref[...], preferred_element_type=F32)
    cv = proj[:, :D_CONV]
    cg = proj[:, D_CONV:2 * D_CONV]
    uc_ref[...] = cv * jax.nn.sigmoid(cg)
    su_ref[...] = proj[:, 2 * D_CONV:]


def _ffn_in_kernel(x_ref, *refs):
    _ffn_in_rows(x_ref, *refs)


def _ffn_in(x, weights, tm):
    rows = x.shape[0]
    assert rows % tm == 0
    row_spec = lambda w: pl.BlockSpec((tm, w), lambda i: (i, 0))
    return pl.pallas_call(
        _ffn_in_kernel,
        grid=(rows // tm,),
        in_specs=[row_spec(D_MODEL)] + [_const_spec(w.shape) for w in weights],
        out_specs=[row_spec(D_MODEL), row_spec(D_CONV), row_spec(D_SSM)],
        out_shape=[jax.ShapeDtypeStruct((rows, D_MODEL), F32),
                   jax.ShapeDtypeStruct((rows, D_CONV), F32),
                   jax.ShapeDtypeStruct((rows, D_SSM), F32)],
        compiler_params=pltpu.CompilerParams(dimension_semantics=("parallel",),
                                             vmem_limit_bytes=VMEM_LIMIT_BYTES),
        name="ffn_in",
    )(x, *weights)


def _ffn_in_prefixed_kernel(*refs, tiles, n_cast):
    x_ref, pre_ref = refs[:2]
    cast_in = refs[2:2 + n_cast]
    weights = refs[2 + n_cast:8 + n_cast]
    x1_ref, uc_ref, su_ref, ucp_ref, sup_ref = refs[8 + n_cast:13 + n_cast]
    cast_out = refs[13 + n_cast:]
    i = pl.program_id(0)

    @pl.when(i < tiles)
    def _():
        for src, dst in zip(cast_in, cast_out):
            dst[...] = src[...].astype(BF16)
        _ffn_in_rows(x_ref, *weights, x1_ref, uc_ref, su_ref)

    @pl.when(i == tiles)
    def _():
        _ffn_in_rows(pre_ref, *weights, None, ucp_ref, sup_ref)


def _ffn_in_prefixed(x, prefix, weights, to_cast, tm):
    rows, pre_rows = x.shape[0], prefix.shape[0]
    assert rows % tm == 0
    tiles = rows // tm
    clamp = lambda i: jnp.minimum(i, tiles - 1)
    row_spec = lambda w: pl.BlockSpec((tm, w), lambda i: (clamp(i), 0))
    pre_spec = lambda w: pl.BlockSpec((pre_rows, w), lambda i: (0, 0))
    slab_specs = []
    for w in to_cast:
        assert w.shape[0] % (tiles * 2 * SUBLANES) == 0, w.shape
        slab_specs.append(pl.BlockSpec((w.shape[0] // tiles, w.shape[1]), lambda i: (clamp(i), 0)))
    return pl.pallas_call(
        functools.partial(_ffn_in_prefixed_kernel, tiles=tiles, n_cast=len(to_cast)),
        grid=(tiles + 1,),
        in_specs=[row_spec(D_MODEL), pre_spec(D_MODEL)] + slab_specs
                 + [_const_spec(w.shape) for w in weights],
        out_specs=[row_spec(D_MODEL), row_spec(D_CONV), row_spec(D_SSM),
                   pre_spec(D_CONV), pre_spec(D_SSM)] + slab_specs,
        out_shape=[jax.ShapeDtypeStruct((rows, D_MODEL), F32),
                   jax.ShapeDtypeStruct((rows, D_CONV), F32),
                   jax.ShapeDtypeStruct((rows, D_SSM), F32),
                   jax.ShapeDtypeStruct((pre_rows, D_CONV), F32),
                   jax.ShapeDtypeStruct((pre_rows, D_SSM), F32)]
                  + [jax.ShapeDtypeStruct(w.shape, BF16) for w in to_cast],
        compiler_params=pltpu.CompilerParams(dimension_semantics=("arbitrary",),
                                             vmem_limit_bytes=VMEM_LIMIT_BYTES),
        name="ffn_in_prefixed",
    )(x, prefix, *to_cast, *weights)


def _ssm_param_kernel(lre_ref, lim_ref, ldt_ref, bre_ref, bim_ref, are_ref, aim_ref, wb_ref):
    lre = lre_ref[...]
    lim = lim_ref[...]
    dt = jnp.exp(ldt_ref[...])
    mag = jnp.exp(lre * dt)
    abr = mag * jnp.cos(lim * dt)
    abi = mag * jnp.sin(lim * dt)
    are_ref[...] = abr
    aim_ref[...] = abi
    nr = abr - 1.0
    den = lre * lre + lim * lim
    cr = (nr * lre + abi * lim) / den
    ci = (abi * lre - nr * lim) / den
    for c in range(SSM_BLOCKS):
        sl = slice(c * BLOCK_STATES, (c + 1) * BLOCK_STATES)
        crc, cic = cr[:, sl], ci[:, sl]
        br, bi = bre_ref[c], bim_ref[c]
        wb_ref[c, :, :BLOCK_STATES] = (crc * br - cic * bi).astype(BF16)
        wb_ref[c, :, BLOCK_STATES:] = (crc * bi + cic * br).astype(BF16)


def _ssm_params(lre, lim, ldt, bre, bim):
    return pl.pallas_call(
        _ssm_param_kernel,
        out_shape=[jax.ShapeDtypeStruct((1, N_STATE), F32),
                   jax.ShapeDtypeStruct((1, N_STATE), F32),
                   jax.ShapeDtypeStruct((SSM_BLOCKS, LANES, 2 * BLOCK_STATES), BF16)],
        name="ssm_params",
    )(lre, lim, ldt, bre, bim)


def _to_step_major(dst_ref, base, src_ref, *, bb, steps):
    for b in range(bb):
        for j in range(src_ref.shape[-1] // LANES):
            dst_ref[j, pl.ds(base + b, steps, stride=bb), :] = src_ref[b, :, j * LANES:(j + 1) * LANES]


def _load_history(ext_ref, cs0_ref, *, bb):
    for k in range(CONV_BLOCKS):
        ext_ref[k, 0:CONV_TAIL * bb, :] = (
            cs0_ref[:, :, k * LANES:(k + 1) * LANES].reshape(CONV_TAIL * bb, LANES))


def _store_history(cs_ref, ext_ref, rows, *, bb):
    for k in range(CONV_BLOCKS):
        cs_ref[:, :, k * LANES:(k + 1) * LANES] = (
            ext_ref[k, rows:rows + CONV_TAIL * bb, :].reshape(CONV_TAIL, bb, LANES))


def _ssm_block(c, su, hbuf_ref, are_ref, aim_ref, wb_ref, hre_ref, him_ref, *, bb, steps):
    st = slice(c * BLOCK_STATES, (c + 1) * BLOCK_STATES)
    hbuf_ref[...] = jnp.dot(su.astype(BF16), wb_ref[c], preferred_element_type=F32)
    ar = jnp.broadcast_to(are_ref[:, st], (SUBLANES, BLOCK_STATES))
    ai = jnp.broadcast_to(aim_ref[:, st], (SUBLANES, BLOCK_STATES))
    for rb in range(bb // SUBLANES):
        seqs = slice(rb * SUBLANES, (rb + 1) * SUBLANES)
        hr, hi = hre_ref[seqs, st], him_ref[seqs, st]
        for t in range(steps):
            rows = slice(t * bb + rb * SUBLANES, t * bb + (rb + 1) * SUBLANES)
            nr = ar * hr - ai * hi + hbuf_ref[rows, :BLOCK_STATES]
            ni = ar * hi + ai * hr + hbuf_ref[rows, BLOCK_STATES:]
            hbuf_ref[rows, :BLOCK_STATES] = nr
            hbuf_ref[rows, BLOCK_STATES:] = ni
            hr, hi = nr, ni
        hre_ref[seqs, st] = hr
        him_ref[seqs, st] = hi


def _mixer_state_kernel(uc_ref, su_ref, cs0_ref, h0re_ref, h0im_ref, are_ref, aim_ref, wb_ref,
                        cs_ref, hre_ref, him_ref, ext_ref, sut_ref, hbuf_ref, *, bb, steps):
    _load_history(ext_ref, cs0_ref, bb=bb)
    _to_step_major(ext_ref, CONV_TAIL * bb, uc_ref, bb=bb, steps=steps)
    _store_history(cs_ref, ext_ref, steps * bb, bb=bb)
    hre_ref[...] = h0re_ref[...]
    him_ref[...] = h0im_ref[...]
    _to_step_major(sut_ref, 0, su_ref, bb=bb, steps=steps)
    for c in range(SSM_BLOCKS):
        _ssm_block(c, sut_ref[c], hbuf_ref, are_ref, aim_ref, wb_ref, hre_ref, him_ref,
                   bb=bb, steps=steps)


def _mixer_state(uc, su, cs0, h0re, h0im, are, aim, wb):
    bb, steps, _ = uc.shape
    rows = bb * steps
    return pl.pallas_call(
        functools.partial(_mixer_state_kernel, bb=bb, steps=steps),
        out_shape=[jax.ShapeDtypeStruct((CONV_TAIL, bb, D_CONV), F32),
                   jax.ShapeDtypeStruct((bb, N_STATE), F32),
                   jax.ShapeDtypeStruct((bb, N_STATE), F32)],
        scratch_shapes=[pltpu.VMEM((CONV_BLOCKS, rows + CONV_TAIL * bb, LANES), F32),
                        pltpu.VMEM((SSM_BLOCKS, rows, LANES), F32),
                        pltpu.VMEM((rows, 2 * BLOCK_STATES), F32)],
        name="mixer_state",
    )(uc, su, cs0, h0re, h0im, are, aim, wb)


def _mixer_ffn_kernel(uc_ref, su_ref, x1_ref, cs0_ref, h0re_ref, h0im_ref,
                      cw_ref, cb_ref, lg_ref, lb_ref, are_ref, aim_ref, wb_ref, wcre_ref, wcim_ref,
                      d_ref, gw_ref, gb_ref, wo_ref,
                      n2_ref, wg_ref, wu_ref, wd_ref, nf_ref,
                      y_ref, cs_ref, hre_ref, him_ref,
                      ext_ref, sut_ref, hbuf_ref, yc_ref, mix_ref, x2_ref, *, bb, steps, chunks):
    rows = steps * bb
    tail_rows = CONV_TAIL * bb
    chunk = pl.program_id(1)

    def ffn():
        x2 = x2_ref[...]
        hb = _rmsnorm(x2, n2_ref[...]).astype(BF16)
        x3 = x2 + 0.5 * _swiglu(hb, wg_ref, wu_ref, wd_ref)
        y_ref[...] = _rmsnorm(x3, nf_ref[...]).reshape(bb, steps, D_MODEL)

    def mixer():
        @pl.when(chunk == 0)
        def _():
            hre_ref[...] = h0re_ref[...]
            him_ref[...] = h0im_ref[...]
            _load_history(ext_ref, cs0_ref, bb=bb)

        _to_step_major(ext_ref, tail_rows, uc_ref, bb=bb, steps=steps)
        _to_step_major(sut_ref, 0, su_ref, bb=bb, steps=steps)

        @pl.when(chunk == chunks - 1)
        def _():
            _store_history(cs_ref, ext_ref, rows, bb=bb)

        for r in range(rows // CONV_ROW_GROUP):
            o = r * CONV_ROW_GROUP
            for k in range(CONV_BLOCKS):
                cols = slice(k * LANES, (k + 1) * LANES)
                accs = [jnp.zeros((SUBLANES, LANES), F32)] * (CONV_ROW_GROUP // SUBLANES)
                for tap in range(CONV_K):
                    wt = cw_ref[tap:tap + 1, cols]
                    accs = [a + wt * ext_ref[k, o + q * SUBLANES + tap * bb:o + (q + 1) * SUBLANES + tap * bb, :]
                            for q, a in enumerate(accs)]
                for q, a in enumerate(accs):
                    yc_ref[k, o + q * SUBLANES:o + (q + 1) * SUBLANES, :] = a + cb_ref[:, cols]

        if chunks > 1:
            ext_ref[:, 0:tail_rows, :] = ext_ref[:, rows:rows + tail_rows, :]

        ycs = [yc_ref[k] for k in range(CONV_BLOCKS)]
        mu = sum(jnp.sum(y, axis=-1, keepdims=True) for y in ycs) * (1.0 / D_CONV)
        cens = [y - mu for y in ycs]
        var = sum(jnp.sum(c * c, axis=-1, keepdims=True) for c in cens) * (1.0 / D_CONV)
        inv = lax.rsqrt(var + LN_EPS)
        acts = []
        for k, cen in enumerate(cens):
            cols = slice(k * LANES, (k + 1) * LANES)
            yn = cen * inv * lg_ref[:, cols] + lb_ref[:, cols]
            acts.append((yn * jax.nn.sigmoid(yn)).astype(BF16))

        zs = []
        for c in range(SSM_BLOCKS):
            su = sut_ref[c]
            hbuf = hbuf_ref.at[c % 2]
            _ssm_block(c, su, hbuf, are_ref, aim_ref, wb_ref, hre_ref, him_ref, bb=bb, steps=steps)
            hb = hbuf[...].astype(BF16)
            y = (jnp.dot(hb[:, :BLOCK_STATES], wcre_ref[c], preferred_element_type=F32)
                 - jnp.dot(hb[:, BLOCK_STATES:], wcim_ref[c], preferred_element_type=F32))
            zs.append(jax.nn.gelu(y + d_ref[:, c * LANES:(c + 1) * LANES] * su))

        z = jnp.concatenate(zs, axis=-1)
        gate = jnp.dot(z.astype(BF16), gw_ref[...], preferred_element_type=F32) + gb_ref[...]
        outs = (z * jax.nn.sigmoid(gate)).astype(BF16)

        mix = jnp.dot(jnp.concatenate(acts + [outs], axis=-1), wo_ref[...], preferred_element_type=F32)
        for k in range(MODEL_BLOCKS):
            mix_ref[k] = mix[:, k * LANES:(k + 1) * LANES]
        for b in range(bb):
            for k in range(MODEL_BLOCKS):
                cols = slice(k * LANES, (k + 1) * LANES)
                x2_ref[b * steps:(b + 1) * steps, cols] = (
                    x1_ref[b, :, cols] + mix_ref[k, pl.ds(b, steps, stride=bb), :])

    mixer()
    ffn()


def _mixer_ffn(uc, su, x1, cs0, h0re, h0im, consts, ffn, *, bb, steps):
    nseq, total, _ = uc.shape
    groups, chunks = nseq // bb, total // steps
    rows, tail_rows = steps * bb, CONV_TAIL * bb
    assert nseq % bb == 0 and total % steps == 0 and bb % SUBLANES == 0 and steps % SUBLANES == 0
    assert rows % CONV_ROW_GROUP == 0 and (chunks == 1 or rows >= tail_rows)
    seq_spec = lambda w: pl.BlockSpec((bb, steps, w), lambda g, i: (g, i, 0))
    state_spec = pl.BlockSpec((bb, N_STATE), lambda g, i: (g, 0))
    cs_spec = pl.BlockSpec((CONV_TAIL, bb, D_CONV), lambda g, i: (0, g, 0))
    return pl.pallas_call(
        functools.partial(_mixer_ffn_kernel, bb=bb, steps=steps, chunks=chunks),
        grid=(groups, chunks),
        in_specs=[seq_spec(D_CONV), seq_spec(D_SSM), seq_spec(D_MODEL), cs_spec, state_spec, state_spec]
                 + [_const_spec(c.shape) for c in consts + ffn],
        out_specs=[seq_spec(D_MODEL), cs_spec, state_spec, state_spec],
        out_shape=[jax.ShapeDtypeStruct((nseq, total, D_MODEL), F32),
                   jax.ShapeDtypeStruct((CONV_TAIL, nseq, D_CONV), F32),
                   jax.ShapeDtypeStruct((nseq, N_STATE), F32),
                   jax.ShapeDtypeStruct((nseq, N_STATE), F32)],
        scratch_shapes=[pltpu.VMEM((CONV_BLOCKS, rows + tail_rows, LANES), F32),
                        pltpu.VMEM((SSM_BLOCKS, rows, LANES), F32),
                        pltpu.VMEM((2, rows, 2 * BLOCK_STATES), F32),
                        pltpu.VMEM((CONV_BLOCKS, rows, LANES), F32),
                        pltpu.VMEM((MODEL_BLOCKS, rows, LANES), F32),
                        pltpu.VMEM((rows, D_MODEL), F32)],
        compiler_params=pltpu.CompilerParams(dimension_semantics=("parallel", "arbitrary"),
                                             vmem_limit_bytes=VMEM_LIMIT_BYTES),
        name="mixer_ffn",
    )(uc, su, x1, cs0, h0re, h0im, *consts, *ffn)


def _block_diag(blocks):
    nb, n, a, b = blocks.shape
    eye = jnp.eye(n, dtype=blocks.dtype)
    return jnp.einsum("cgab,gh->cgahb", blocks, eye).reshape(nb, n * a, n * b)


PROMPT_STEPS = 64
PROMPT_FFN_ROWS = 1024
SAMPLE_SEQS = 32
SAMPLE_FFN_ROWS = 512


def kernel(x_prompt, x_sample, state_conv, state_ssm_re, state_ssm_im, meta_tokens, ffn1_norm, ffn1_w_gate, ffn1_w_up, ffn1_w_down, mix_norm, w_in, conv_w, conv_b, conv_ln_g, conv_ln_b, ssm_lambda_re, ssm_lambda_im, ssm_log_dt, ssm_b_re, ssm_b_im, ssm_c_re, ssm_c_im, ssm_d, ssm_glu_w, ssm_glu_b, w_out, ffn2_norm, ffn2_w_gate, ffn2_w_up, ffn2_w_down, final_norm):
    assert ffn1_norm.shape[0] == 1, "one layer"
    bp, lp, _ = x_prompt.shape
    bs, ls, _ = x_sample.shape
    groups_per_block = SSM_GROUPS // SSM_BLOCKS

    row = lambda v: v.reshape(1, -1).astype(F32)
    w1 = (row(ffn1_norm[0]), ffn1_w_gate[0].astype(BF16), ffn1_w_up[0].astype(BF16),
          ffn1_w_down[0].astype(BF16), row(mix_norm[0]), w_in[0].astype(BF16))
    b_blocks = lambda b: _block_diag(
        b.reshape(SSM_BLOCKS, groups_per_block, SSM_P, SSM_GC).transpose(0, 1, 3, 2))
    c_blocks = lambda c: _block_diag(
        c.reshape(SSM_BLOCKS, groups_per_block, SSM_GC, SSM_P).transpose(0, 1, 3, 2)).astype(BF16)
    are, aim, wb = _ssm_params(row(ssm_lambda_re[0]), row(ssm_lambda_im[0]),
                               row(jnp.repeat(ssm_log_dt[0], SSM_P)),
                               b_blocks(ssm_b_re[0]), b_blocks(ssm_b_im[0]))

    meta = jnp.broadcast_to(meta_tokens.astype(x_prompt.dtype)[None], (bp, N_META, D_MODEL))
    x1p, ucp, sup, ucm, sum_, w2g, w2u, w2d, wo, gw = _ffn_in_prefixed(
        x_prompt.reshape(bp * lp, D_MODEL), meta.reshape(bp * N_META, D_MODEL), w1,
        (ffn2_w_gate[0], ffn2_w_up[0], ffn2_w_down[0], w_out[0], ssm_glu_w[0]), PROMPT_FFN_ROWS)
    w2 = (row(ffn2_norm[0]), w2g, w2u, w2d, row(final_norm))
    consts = (conv_w[0], row(conv_b[0]), row(conv_ln_g[0]), row(conv_ln_b[0]), are, aim, wb,
              c_blocks(ssm_c_re[0]), c_blocks(ssm_c_im[0]), row(ssm_d[0]),
              gw, row(ssm_glu_b[0]), wo)
    cs_m, hre_m, him_m = _mixer_state(
        ucm.reshape(bp, N_META, D_CONV), sum_.reshape(bp, N_META, D_SSM),
        jnp.zeros((CONV_TAIL, bp, D_CONV), F32), jnp.zeros((bp, N_STATE), F32),
        jnp.zeros((bp, N_STATE), F32), are, aim, wb)
    y_prompt, cs_p, hre_p, him_p = _mixer_ffn(
        ucp.reshape(bp, lp, D_CONV), sup.reshape(bp, lp, D_SSM), x1p.reshape(bp, lp, D_MODEL),
        cs_m, hre_m, him_m, consts, w2, bb=bp, steps=PROMPT_STEPS)

    x1s, ucs, sus = _ffn_in(x_sample.reshape(bs * ls, D_MODEL), w1, SAMPLE_FFN_ROWS)
    y_sample, cs_s, hre_s, him_s = _mixer_ffn(
        ucs.reshape(bs, ls, D_CONV), sus.reshape(bs, ls, D_SSM), x1s.reshape(bs, ls, D_MODEL),
        state_conv[0].transpose(1, 0, 2), state_ssm_re[0].reshape(bs, N_STATE),
        state_ssm_im[0].reshape(bs, N_STATE), consts, w2, bb=SAMPLE_SEQS, steps=ls)

    state = lambda h, b: h.reshape(1, b, SSM_GROUPS, SSM_P)
    history = lambda cs: cs.transpose(1, 0, 2)[None]
    return (y_prompt, y_sample, history(cs_p), state(hre_p, bp), state(him_p, bp),
            history(cs_s), state(hre_s, bs), state(him_s, bs))
```

```python
import functools

import jax
import jax.numpy as jnp
from jax import lax
from jax.experimental import pallas as pl
from jax.experimental.pallas import tpu as pltpu

F32 = jnp.float32
BF16 = jnp.bfloat16

D_MODEL = 1024
D_FF = 2816
D_CONV = 512
D_SSM = 512
CONV_K = 31
CONV_TAIL = CONV_K - 1
N_META = 16
SSM_GROUPS = 32
SSM_GC = 16
SSM_P = 64
N_STATE = SSM_GROUPS * SSM_P
RMS_EPS = 1e-6
LN_EPS = 1e-5

SUBLANES = 8
LANES = 128
FF_CHUNK = 256
N_FF_CHUNKS = D_FF // FF_CHUNK
CONV_BLOCKS = D_CONV // LANES
SSM_BLOCKS = D_SSM // LANES
MODEL_BLOCKS = D_MODEL // LANES
BLOCK_STATES = N_STATE // SSM_BLOCKS
CONV_ROW_GROUP = 8 * SUBLANES
VMEM_LIMIT_BYTES = 60 * 1024 * 1024


def _const_spec(shape):
    zeros = (0,) * len(shape)
    return pl.BlockSpec(shape, lambda *_: zeros, pipeline_mode=pl.Buffered(1))


def _rmsnorm(x, g):
    return x * lax.rsqrt(jnp.mean(x * x, axis=-1, keepdims=True) + RMS_EPS) * g


def _swiglu(hb, wg_ref, wu_ref, wd_ref):
    acc = jnp.zeros((hb.shape[0], D_MODEL), F32)
    for c in range(N_FF_CHUNKS):
        cols = slice(c * FF_CHUNK, (c + 1) * FF_CHUNK)
        g = jnp.dot(hb, wg_ref[:, cols], preferred_element_type=F32)
        u = jnp.dot(hb, wu_ref[:, cols], preferred_element_type=F32)
        a = (g * jax.nn.sigmoid(g) * u).astype(BF16)
        acc = acc + jnp.dot(a, wd_ref[cols, :], preferred_element_type=F32)
    return acc


def _store_step_major(dst_ref, val, *, nseq, steps, bb):
    for s in range(nseq):
        base = (s // bb) * (bb * steps) + s % bb
        for k in range(val.shape[-1] // LANES):
            dst_ref[k, pl.ds(base, steps, stride=bb), :] = (
                val[s * steps:(s + 1) * steps, k * LANES:(k + 1) * LANES])


def _ffn_in_rows(x_ref, n1_ref, wg_ref, wu_ref, wd_ref, n2_ref, win_ref, x1_ref, uc_ref, su_ref, *, bb):
    nseq, steps, _ = x_ref.shape
    x = x_ref[...].reshape(nseq * steps, D_MODEL)
    hb = _rmsnorm(x, n1_ref[...]).astype(BF16)
    x1 = x + 0.5 * _swiglu(hb, wg_ref, wu_ref, wd_ref)
    if x1_ref is not None:
        x1_ref[...] = x1.reshape(nseq, steps, D_MODEL)
    h2 = _rmsnorm(x1, n2_ref[...]).astype(BF16)
    proj = jnp.dot(h2, win_ref[...], preferred_element_type=F32)
    cv = proj[:, :D_CONV]
    cg = proj[:, D_CONV:2 * D_CONV]
    _store_step_major(uc_ref, cv * jax.nn.sigmoid(cg), nseq=nseq, steps=steps, bb=bb)
    _store_step_major(su_ref, proj[:, 2 * D_CONV:], nseq=nseq, steps=steps, bb=bb)


def _ffn_in_kernel(x_ref, *refs, bb):
    _ffn_in_rows(x_ref, *refs, bb=bb)


def _ffn_in(x, weights, *, seqs, bb):
    nseq, steps, _ = x.shape
    assert nseq % seqs == 0 and seqs % bb == 0
    rows = seqs * steps
    mix_spec = pl.BlockSpec((CONV_BLOCKS, rows, LANES), lambda i: (0, i, 0))
    mix_shape = jax.ShapeDtypeStruct((CONV_BLOCKS, nseq * steps, LANES), F32)
    return pl.pallas_call(
        functools.partial(_ffn_in_kernel, bb=bb),
        grid=(nseq // seqs,),
        in_specs=[pl.BlockSpec((seqs, steps, D_MODEL), lambda i: (i, 0, 0))]
                 + [_const_spec(w.shape) for w in weights],
        out_specs=[pl.BlockSpec((seqs, steps, D_MODEL), lambda i: (i, 0, 0)), mix_spec, mix_spec],
        out_shape=[jax.ShapeDtypeStruct((nseq, steps, D_MODEL), F32), mix_shape, mix_shape],
        compiler_params=pltpu.CompilerParams(dimension_semantics=("parallel",),
                                             vmem_limit_bytes=VMEM_LIMIT_BYTES),
        name="ffn_in",
    )(x, *weights)


def _ffn_in_prefixed_kernel(*refs, tiles, n_cast, bb):
    x_ref, pre_ref = refs[:2]
    cast_in = refs[2:2 + n_cast]
    weights = refs[2 + n_cast:8 + n_cast]
    x1_ref, uc_ref, su_ref, ucp_ref, sup_ref = refs[8 + n_cast:13 + n_cast]
    cast_out = refs[13 + n_cast:]
    i = pl.program_id(0)

    @pl.when(i < tiles)
    def _():
        for src, dst in zip(cast_in, cast_out):
            dst[...] = src[...].astype(BF16)
        _ffn_in_rows(x_ref, *weights, x1_ref, uc_ref, su_ref, bb=bb)

    @pl.when(i == tiles)
    def _():
        _ffn_in_rows(pre_ref, *weights, None, ucp_ref, sup_ref, bb=bb)


def _ffn_in_prefixed(x, prefix, weights, to_cast, *, steps):
    nseq, total, _ = x.shape
    pre_steps = prefix.shape[1]
    assert total % steps == 0
    tiles = total // steps
    clamp = lambda i: jnp.minimum(i, tiles - 1)
    mix_spec = pl.BlockSpec((CONV_BLOCKS, nseq * steps, LANES), lambda i: (0, clamp(i), 0))
    mix_shape = jax.ShapeDtypeStruct((CONV_BLOCKS, nseq * total, LANES), F32)
    pre_spec = pl.BlockSpec((CONV_BLOCKS, nseq * pre_steps, LANES), lambda i: (0, 0, 0))
    pre_shape = jax.ShapeDtypeStruct((CONV_BLOCKS, nseq * pre_steps, LANES), F32)
    slab_specs = []
    for w in to_cast:
        assert w.shape[0] % (tiles * 2 * SUBLANES) == 0, w.shape
        slab_specs.append(pl.BlockSpec((w.shape[0] // tiles, w.shape[1]), lambda i: (clamp(i), 0)))
    return pl.pallas_call(
        functools.partial(_ffn_in_prefixed_kernel, tiles=tiles, n_cast=len(to_cast), bb=nseq),
        grid=(tiles + 1,),
        in_specs=[pl.BlockSpec((nseq, steps, D_MODEL), lambda i: (0, clamp(i), 0)),
                  pl.BlockSpec(prefix.shape, lambda i: (0, 0, 0))] + slab_specs
                 + [_const_spec(w.shape) for w in weights],
        out_specs=[pl.BlockSpec((nseq, steps, D_MODEL), lambda i: (0, clamp(i), 0)),
                   mix_spec, mix_spec, pre_spec, pre_spec] + slab_specs,
        out_shape=[jax.ShapeDtypeStruct((nseq, total, D_MODEL), F32), mix_shape, mix_shape,
                   pre_shape, pre_shape] + [jax.ShapeDtypeStruct(w.shape, BF16) for w in to_cast],
        compiler_params=pltpu.CompilerParams(dimension_semantics=("arbitrary",),
                                             vmem_limit_bytes=VMEM_LIMIT_BYTES),
        name="ffn_in_prefixed",
    )(x, prefix, *to_cast, *weights)


def _ssm_param_kernel(lre_ref, lim_ref, ldt_ref, bre_ref, bim_ref, are_ref, aim_ref, wb_ref):
    lre = lre_ref[...]
    lim = lim_ref[...]
    dt = jnp.exp(ldt_ref[...])
    mag = jnp.exp(lre * dt)
    abr = mag * jnp.cos(lim * dt)
    abi = mag * jnp.sin(lim * dt)
    are_ref[...] = abr
    aim_ref[...] = abi
    nr = abr - 1.0
    den = lre * lre + lim * lim
    cr = (nr * lre + abi * lim) / den
    ci = (abi * lre - nr * lim) / den
    for c in range(SSM_BLOCKS):
        sl = slice(c * BLOCK_STATES, (c + 1) * BLOCK_STATES)
        crc, cic = cr[:, sl], ci[:, sl]
        br, bi = bre_ref[c], bim_ref[c]
        wb_ref[c, :, :BLOCK_STATES] = (crc * br - cic * bi).astype(BF16)
        wb_ref[c, :, BLOCK_STATES:] = (crc * bi + cic * br).astype(BF16)


def _ssm_params(lre, lim, ldt, bre, bim):
    return pl.pallas_call(
        _ssm_param_kernel,
        out_shape=[jax.ShapeDtypeStruct((1, N_STATE), F32),
                   jax.ShapeDtypeStruct((1, N_STATE), F32),
                   jax.ShapeDtypeStruct((SSM_BLOCKS, LANES, 2 * BLOCK_STATES), BF16)],
        name="ssm_params",
    )(lre, lim, ldt, bre, bim)


def _load_history(tail_ref, cs0_ref, *, bb):
    for k in range(CONV_BLOCKS):
        tail_ref[k] = cs0_ref[:, :, k * LANES:(k + 1) * LANES].reshape(CONV_TAIL * bb, LANES)


def _advance_history(tail_ref, uc_ref, rows, *, bb):
    tail_rows = CONV_TAIL * bb
    if rows >= tail_rows:
        tail_ref[...] = uc_ref[:, rows - tail_rows:rows, :]
    else:
        tail_ref[:, 0:tail_rows - rows, :] = tail_ref[:, rows:tail_rows, :]
        tail_ref[:, tail_rows - rows:tail_rows, :] = uc_ref[...]


def _store_history(cs_ref, tail_ref, *, bb):
    for k in range(CONV_BLOCKS):
        cs_ref[:, :, k * LANES:(k + 1) * LANES] = tail_ref[k].reshape(CONV_TAIL, bb, LANES)


def _ssm_block(c, su, hbuf_ref, are_ref, aim_ref, wb_ref, hre_ref, him_ref, *, bb, steps):
    st = slice(c * BLOCK_STATES, (c + 1) * BLOCK_STATES)
    hbuf_ref[...] = jnp.dot(su.astype(BF16), wb_ref[c], preferred_element_type=F32)
    ar = jnp.broadcast_to(are_ref[:, st], (SUBLANES, BLOCK_STATES))
    ai = jnp.broadcast_to(aim_ref[:, st], (SUBLANES, BLOCK_STATES))
    for rb in range(bb // SUBLANES):
        seqs = slice(rb * SUBLANES, (rb + 1) * SUBLANES)
        hr, hi = hre_ref[seqs, st], him_ref[seqs, st]
        for t in range(steps):
            rows = slice(t * bb + rb * SUBLANES, t * bb + (rb + 1) * SUBLANES)
            nr = ar * hr - ai * hi + hbuf_ref[rows, :BLOCK_STATES]
            ni = ar * hi + ai * hr + hbuf_ref[rows, BLOCK_STATES:]
            hbuf_ref[rows, :BLOCK_STATES] = nr
            hbuf_ref[rows, BLOCK_STATES:] = ni
            hr, hi = nr, ni
        hre_ref[seqs, st] = hr
        him_ref[seqs, st] = hi


def _mixer_state_kernel(uc_ref, su_ref, cs0_ref, h0re_ref, h0im_ref, are_ref, aim_ref, wb_ref,
                        cs_ref, hre_ref, him_ref, tail_ref, hbuf_ref, *, bb, steps):
    _load_history(tail_ref, cs0_ref, bb=bb)
    _advance_history(tail_ref, uc_ref, steps * bb, bb=bb)
    _store_history(cs_ref, tail_ref, bb=bb)
    hre_ref[...] = h0re_ref[...]
    him_ref[...] = h0im_ref[...]
    for c in range(SSM_BLOCKS):
        _ssm_block(c, su_ref[c], hbuf_ref, are_ref, aim_ref, wb_ref, hre_ref, him_ref,
                   bb=bb, steps=steps)


def _mixer_state(uc, su, cs0, h0re, h0im, are, aim, wb):
    bb = cs0.shape[1]
    rows = uc.shape[1]
    return pl.pallas_call(
        functools.partial(_mixer_state_kernel, bb=bb, steps=rows // bb),
        out_shape=[jax.ShapeDtypeStruct((CONV_TAIL, bb, D_CONV), F32),
                   jax.ShapeDtypeStruct((bb, N_STATE), F32),
                   jax.ShapeDtypeStruct((bb, N_STATE), F32)],
        scratch_shapes=[pltpu.VMEM((CONV_BLOCKS, CONV_TAIL * bb, LANES), F32),
                        pltpu.VMEM((rows, 2 * BLOCK_STATES), F32)],
        name="mixer_state",
    )(uc, su, cs0, h0re, h0im, are, aim, wb)


def _mixer_ffn_kernel(uc_ref, su_ref, x1_ref, cs0_ref, h0re_ref, h0im_ref,
                      cw_ref, cb_ref, lg_ref, lb_ref, are_ref, aim_ref, wb_ref, wcre_ref, wcim_ref,
                      d_ref, gw_ref, gb_ref, wo_ref,
                      n2_ref, wg_ref, wu_ref, wd_ref, nf_ref,
                      y_ref, cs_ref, hre_ref, him_ref,
                      tail_ref, hbuf_ref, yc_ref, mix_ref, x2_ref, *, bb, steps, chunks):
    rows = steps * bb
    tail_rows = CONV_TAIL * bb
    chunk = pl.program_id(1)

    @pl.when(chunk == 0)
    def _():
        hre_ref[...] = h0re_ref[...]
        him_ref[...] = h0im_ref[...]
        _load_history(tail_ref, cs0_ref, bb=bb)

    def history_or_chunk(k, row):
        if row < tail_rows:
            return tail_ref[k, row:row + SUBLANES, :]
        return uc_ref[k, row - tail_rows:row - tail_rows + SUBLANES, :]

    for r in range(rows // CONV_ROW_GROUP):
        o = r * CONV_ROW_GROUP
        for k in range(CONV_BLOCKS):
            cols = slice(k * LANES, (k + 1) * LANES)
            accs = [jnp.zeros((SUBLANES, LANES), F32)] * (CONV_ROW_GROUP // SUBLANES)
            for tap in range(CONV_K):
                wt = cw_ref[tap:tap + 1, cols]
                accs = [a + wt * history_or_chunk(k, o + q * SUBLANES + tap * bb)
                        for q, a in enumerate(accs)]
            for q, a in enumerate(accs):
                yc_ref[k, o + q * SUBLANES:o + (q + 1) * SUBLANES, :] = a + cb_ref[:, cols]

    _advance_history(tail_ref, uc_ref, rows, bb=bb)

    ycs = [yc_ref[k] for k in range(CONV_BLOCKS)]
    mu = sum(jnp.sum(y, axis=-1, keepdims=True) for y in ycs) * (1.0 / D_CONV)
    cens = [y - mu for y in ycs]
    var = sum(jnp.sum(c * c, axis=-1, keepdims=True) for c in cens) * (1.0 / D_CONV)
    inv = lax.rsqrt(var + LN_EPS)
    acts = []
    for k, cen in enumerate(cens):
        cols = slice(k * LANES, (k + 1) * LANES)
        yn = cen * inv * lg_ref[:, cols] + lb_ref[:, cols]
        acts.append((yn * jax.nn.sigmoid(yn)).astype(BF16))

    zs = []
    for c in range(SSM_BLOCKS):
        su = su_ref[c]
        hbuf = hbuf_ref.at[c % 2]
        _ssm_block(c, su, hbuf, are_ref, aim_ref, wb_ref, hre_ref, him_ref, bb=bb, steps=steps)
        hb = hbuf[...].astype(BF16)
        y = (jnp.dot(hb[:, :BLOCK_STATES], wcre_ref[c], preferred_element_type=F32)
             - jnp.dot(hb[:, BLOCK_STATES:], wcim_ref[c], preferred_element_type=F32))
        zs.append(jax.nn.gelu(y + d_ref[:, c * LANES:(c + 1) * LANES] * su))

    z = jnp.concatenate(zs, axis=-1)
    gate = jnp.dot(z.astype(BF16), gw_ref[...], preferred_element_type=F32) + gb_ref[...]
    outs = (z * jax.nn.sigmoid(gate)).astype(BF16)

    mix = jnp.dot(jnp.concatenate(acts + [outs], axis=-1), wo_ref[...], preferred_element_type=F32)
    for k in range(MODEL_BLOCKS):
        mix_ref[k] = mix[:, k * LANES:(k + 1) * LANES]
    for b in range(bb):
        for k in range(MODEL_BLOCKS):
            cols = slice(k * LANES, (k + 1) * LANES)
            x2_ref[b * steps:(b + 1) * steps, cols] = (
                x1_ref[b, :, cols] + mix_ref[k, pl.ds(b, steps, stride=bb), :])

    x2 = x2_ref[...]
    hb2 = _rmsnorm(x2, n2_ref[...]).astype(BF16)
    x3 = x2 + 0.5 * _swiglu(hb2, wg_ref, wu_ref, wd_ref)
    y_ref[...] = _rmsnorm(x3, nf_ref[...]).reshape(bb, steps, D_MODEL)

    @pl.when(chunk == chunks - 1)
    def _():
        _store_history(cs_ref, tail_ref, bb=bb)


def _mixer_ffn(uc, su, x1, cs0, h0re, h0im, consts, ffn, *, bb, steps):
    nseq, total, _ = x1.shape
    groups, chunks = nseq // bb, total // steps
    rows, tail_rows = steps * bb, CONV_TAIL * bb
    assert nseq % bb == 0 and total % steps == 0 and bb % SUBLANES == 0 and steps % SUBLANES == 0
    assert rows % CONV_ROW_GROUP == 0 and (groups == 1 or chunks == 1)
    seq_spec = pl.BlockSpec((bb, steps, D_MODEL), lambda g, i: (g, i, 0))
    mix_spec = pl.BlockSpec((CONV_BLOCKS, rows, LANES), lambda g, i: (0, g * chunks + i, 0))
    state_spec = pl.BlockSpec((bb, N_STATE), lambda g, i: (g, 0))
    cs_spec = pl.BlockSpec((CONV_TAIL, bb, D_CONV), lambda g, i: (0, g, 0))
    return pl.pallas_call(
        functools.partial(_mixer_ffn_kernel, bb=bb, steps=steps, chunks=chunks),
        grid=(groups, chunks),
        in_specs=[mix_spec, mix_spec, seq_spec, cs_spec, state_spec, state_spec]
                 + [_const_spec(c.shape) for c in consts + ffn],
        out_specs=[seq_spec, cs_spec, state_spec, state_spec],
        out_shape=[jax.ShapeDtypeStruct((nseq, total, D_MODEL), F32),
                   jax.ShapeDtypeStruct((CONV_TAIL, nseq, D_CONV), F32),
                   jax.ShapeDtypeStruct((nseq, N_STATE), F32),
                   jax.ShapeDtypeStruct((nseq, N_STATE), F32)],
        scratch_shapes=[pltpu.VMEM((CONV_BLOCKS, tail_rows, LANES), F32),
                        pltpu.VMEM((2, rows, 2 * BLOCK_STATES), F32),
                        pltpu.VMEM((CONV_BLOCKS, rows, LANES), F32),
                        pltpu.VMEM((MODEL_BLOCKS, rows, LANES), F32),
                        pltpu.VMEM((rows, D_MODEL), F32)],
        compiler_params=pltpu.CompilerParams(dimension_semantics=("parallel", "arbitrary"),
                                             vmem_limit_bytes=VMEM_LIMIT_BYTES),
        name="mixer_ffn",
    )(uc, su, x1, cs0, h0re, h0im, *consts, *ffn)


def _block_diag(blocks):
    nb, n, a, b = blocks.shape
    eye = jnp.eye(n, dtype=blocks.dtype)
    return jnp.einsum("cgab,gh->cgahb", blocks, eye).reshape(nb, n * a, n * b)


PROMPT_STEPS = 64
PROMPT_FFN_STEPS = 128
SAMPLE_SEQS = 32
SAMPLE_FFN_SEQS = 64


def kernel(x_prompt, x_sample, state_conv, state_ssm_re, state_ssm_im, meta_tokens, ffn1_norm, ffn1_w_gate, ffn1_w_up, ffn1_w_down, mix_norm, w_in, conv_w, conv_b, conv_ln_g, conv_ln_b, ssm_lambda_re, ssm_lambda_im, ssm_log_dt, ssm_b_re, ssm_b_im, ssm_c_re, ssm_c_im, ssm_d, ssm_glu_w, ssm_glu_b, w_out, ffn2_norm, ffn2_w_gate, ffn2_w_up, ffn2_w_down, final_norm):
    assert ffn1_norm.shape[0] == 1, "one layer"
    bp, lp, _ = x_prompt.shape
    bs, ls, _ = x_sample.shape
    groups_per_block = SSM_GROUPS // SSM_BLOCKS

    row = lambda v: v.reshape(1, -1).astype(F32)
    w1 = (row(ffn1_norm[0]), ffn1_w_gate[0].astype(BF16), ffn1_w_up[0].astype(BF16),
          ffn1_w_down[0].astype(BF16), row(mix_norm[0]), w_in[0].astype(BF16))
    b_blocks = lambda b: _block_diag(
        b.reshape(SSM_BLOCKS, groups_per_block, SSM_P, SSM_GC).transpose(0, 1, 3, 2))
    c_blocks = lambda c: _block_diag(
        c.reshape(SSM_BLOCKS, groups_per_block, SSM_GC, SSM_P).transpose(0, 1, 3, 2)).astype(BF16)
    are, aim, wb = _ssm_params(row(ssm_lambda_re[0]), row(ssm_lambda_im[0]),
                               row(jnp.repeat(ssm_log_dt[0], SSM_P)),
                               b_blocks(ssm_b_re[0]), b_blocks(ssm_b_im[0]))

    meta = jnp.broadcast_to(meta_tokens.astype(x_prompt.dtype)[None], (bp, N_META, D_MODEL))
    x1p, ucp, sup, ucm, sum_, w2g, w2u, w2d, wo, gw = _ffn_in_prefixed(
        x_prompt, meta, w1, (ffn2_w_gate[0], ffn2_w_up[0], ffn2_w_down[0], w_out[0], ssm_glu_w[0]),
        steps=PROMPT_FFN_STEPS)
    w2 = (row(ffn2_norm[0]), w2g, w2u, w2d, row(final_norm))
    consts = (conv_w[0], row(conv_b[0]), row(conv_ln_g[0]), row(conv_ln_b[0]), are, aim, wb,
              c_blocks(ssm_c_re[0]), c_blocks(ssm_c_im[0]), row(ssm_d[0]),
              gw, row(ssm_glu_b[0]), wo)
    cs_m, hre_m, him_m = _mixer_state(
        ucm, sum_, jnp.zeros((CONV_TAIL, bp, D_CONV), F32), jnp.zeros((bp, N_STATE), F32),
        jnp.zeros((bp, N_STATE), F32), are, aim, wb)
    y_prompt, cs_p, hre_p, him_p = _mixer_ffn(ucp, sup, x1p, cs_m, hre_m, him_m, consts, w2,
                                              bb=bp, steps=PROMPT_STEPS)

    x1s, ucs, sus = _ffn_in(x_sample, w1, seqs=SAMPLE_FFN_SEQS, bb=SAMPLE_SEQS)
    y_sample, cs_s, hre_s, him_s = _mixer_ffn(
        ucs, sus, x1s, state_conv[0].transpose(1, 0, 2), state_ssm_re[0].reshape(bs, N_STATE),
        state_ssm_im[0].reshape(bs, N_STATE), consts, w2, bb=SAMPLE_SEQS, steps=ls)

    state = lambda h, b: h.reshape(1, b, SSM_GROUPS, SSM_P)
    history = lambda cs: cs.transpose(1, 0, 2)[None]
    return (y_prompt, y_sample, history(cs_p), state(hre_p, bp), state(him_p, bp),
            history(cs_s), state(hre_s, bs), state(him_s, bs))
```

```python
import functools

import jax
import jax.numpy as jnp
from jax import lax
from jax.experimental import pallas as pl
from jax.experimental.pallas import tpu as pltpu

F32 = jnp.float32
BF16 = jnp.bfloat16

D_MODEL = 1024
D_FF = 2816
D_CONV = 512
D_SSM = 512
CONV_K = 31
CONV_TAIL = CONV_K - 1
N_META = 16
SSM_GROUPS = 32
SSM_GC = 16
SSM_P = 64
N_STATE = SSM_GROUPS * SSM_P
RMS_EPS = 1e-6
LN_EPS = 1e-5

SUBLANES = 8
LANES = 128
FF_CHUNK = 256
N_FF_CHUNKS = D_FF // FF_CHUNK
CONV_BLOCKS = D_CONV // LANES
SSM_BLOCKS = D_SSM // LANES
MODEL_BLOCKS = D_MODEL // LANES
BLOCK_STATES = N_STATE // SSM_BLOCKS
CONV_ROW_GROUP = 8 * SUBLANES
VMEM_LIMIT_BYTES = 60 * 1024 * 1024


def _const_spec(shape):
    zeros = (0,) * len(shape)
    return pl.BlockSpec(shape, lambda *_: zeros, pipeline_mode=pl.Buffered(1))


def _rmsnorm(x, g):
    return x * lax.rsqrt(jnp.mean(x * x, axis=-1, keepdims=True) + RMS_EPS) * g


def _swiglu(hb, wg_ref, wu_ref, wd_ref):
    acc = jnp.zeros((hb.shape[0], D_MODEL), F32)
    for c in range(N_FF_CHUNKS):
        cols = slice(c * FF_CHUNK, (c + 1) * FF_CHUNK)
        g = jnp.dot(hb, wg_ref[:, cols], preferred_element_type=F32)
        u = jnp.dot(hb, wu_ref[:, cols], preferred_element_type=F32)
        a = (g * jax.nn.sigmoid(g) * u).astype(BF16)
        acc = acc + jnp.dot(a, wd_ref[cols, :], preferred_element_type=F32)
    return acc


def _store_step_major(dst_ref, val, *, nseq, steps, bb):
    for s in range(nseq):
        base = (s // bb) * (bb * steps) + s % bb
        for k in range(val.shape[-1] // LANES):
            dst_ref[k, pl.ds(base, steps, stride=bb), :] = (
                val[s * steps:(s + 1) * steps, k * LANES:(k + 1) * LANES])


def _ffn_in_rows(x_ref, n1_ref, wg_ref, wu_ref, wd_ref, n2_ref, win_ref, x1_ref, uc_ref, su_ref, *, bb):
    nseq, steps, _ = x_ref.shape
    x = x_ref[...].reshape(nseq * steps, D_MODEL)
    hb = _rmsnorm(x, n1_ref[...]).astype(BF16)
    x1 = x + 0.5 * _swiglu(hb, wg_ref, wu_ref, wd_ref)
    if x1_ref is not None:
        x1_ref[...] = x1.reshape(nseq, steps, D_MODEL)
    h2 = _rmsnorm(x1, n2_ref[...]).astype(BF16)
    proj = jnp.dot(h2, win_ref[...], preferred_element_type=F32)
    cv = proj[:, :D_CONV]
    cg = proj[:, D_CONV:2 * D_CONV]
    _store_step_major(uc_ref, cv * jax.nn.sigmoid(cg), nseq=nseq, steps=steps, bb=bb)
    _store_step_major(su_ref, proj[:, 2 * D_CONV:], nseq=nseq, steps=steps, bb=bb)


def _ffn_in_kernel(x_ref, *refs, bb):
    _ffn_in_rows(x_ref, *refs, bb=bb)


def _ffn_in(x, weights, *, seqs, bb):
    nseq, steps, _ = x.shape
    assert nseq % seqs == 0 and seqs % bb == 0
    rows = seqs * steps
    mix_spec = pl.BlockSpec((CONV_BLOCKS, rows, LANES), lambda i: (0, i, 0))
    mix_shape = jax.ShapeDtypeStruct((CONV_BLOCKS, nseq * steps, LANES), F32)
    return pl.pallas_call(
        functools.partial(_ffn_in_kernel, bb=bb),
        grid=(nseq // seqs,),
        in_specs=[pl.BlockSpec((seqs, steps, D_MODEL), lambda i: (i, 0, 0))]
                 + [_const_spec(w.shape) for w in weights],
        out_specs=[pl.BlockSpec((seqs, steps, D_MODEL), lambda i: (i, 0, 0)), mix_spec, mix_spec],
        out_shape=[jax.ShapeDtypeStruct((nseq, steps, D_MODEL), F32), mix_shape, mix_shape],
        compiler_params=pltpu.CompilerParams(dimension_semantics=("parallel",),
                                             vmem_limit_bytes=VMEM_LIMIT_BYTES),
        name="ffn_in",
    )(x, *weights)


def _ffn_in_prefixed_kernel(*refs, tiles, n_cast, bb):
    x_ref, pre_ref = refs[:2]
    cast_in = refs[2:2 + n_cast]
    weights = refs[2 + n_cast:8 + n_cast]
    x1_ref, uc_ref, su_ref, ucp_ref, sup_ref = refs[8 + n_cast:13 + n_cast]
    cast_out = refs[13 + n_cast:]
    i = pl.program_id(0)

    @pl.when(i < tiles)
    def _():
        for src, dst in zip(cast_in, cast_out):
            dst[...] = src[...].astype(BF16)
        _ffn_in_rows(x_ref, *weights, x1_ref, uc_ref, su_ref, bb=bb)

    @pl.when(i == tiles)
    def _():
        _ffn_in_rows(pre_ref, *weights, None, ucp_ref, sup_ref, bb=bb)


def _ffn_in_prefixed(x, prefix, weights, to_cast, *, steps):
    nseq, total, _ = x.shape
    pre_steps = prefix.shape[1]
    assert total % steps == 0
    tiles = total // steps
    clamp = lambda i: jnp.minimum(i, tiles - 1)
    mix_spec = pl.BlockSpec((CONV_BLOCKS, nseq * steps, LANES), lambda i: (0, clamp(i), 0))
    mix_shape = jax.ShapeDtypeStruct((CONV_BLOCKS, nseq * total, LANES), F32)
    pre_spec = pl.BlockSpec((CONV_BLOCKS, nseq * pre_steps, LANES), lambda i: (0, 0, 0))
    pre_shape = jax.ShapeDtypeStruct((CONV_BLOCKS, nseq * pre_steps, LANES), F32)
    slab_specs = []
    for w in to_cast:
        assert w.shape[0] % (tiles * 2 * SUBLANES) == 0, w.shape
        slab_specs.append(pl.BlockSpec((w.shape[0] // tiles, w.shape[1]), lambda i: (clamp(i), 0)))
    return pl.pallas_call(
        functools.partial(_ffn_in_prefixed_kernel, tiles=tiles, n_cast=len(to_cast), bb=nseq),
        grid=(tiles + 1,),
        in_specs=[pl.BlockSpec((nseq, steps, D_MODEL), lambda i: (0, clamp(i), 0)),
                  pl.BlockSpec(prefix.shape, lambda i: (0, 0, 0))] + slab_specs
                 + [_const_spec(w.shape) for w in weights],
        out_specs=[pl.BlockSpec((nseq, steps, D_MODEL), lambda i: (0, clamp(i), 0)),
                   mix_spec, mix_spec, pre_spec, pre_spec] + slab_specs,
        out_shape=[jax.ShapeDtypeStruct((nseq, total, D_MODEL), F32), mix_shape, mix_shape,
                   pre_shape, pre_shape] + [jax.ShapeDtypeStruct(w.shape, BF16) for w in to_cast],
        compiler_params=pltpu.CompilerParams(dimension_semantics=("arbitrary",),
                                             vmem_limit_bytes=VMEM_LIMIT_BYTES),
        name="ffn_in_prefixed",
    )(x, prefix, *to_cast, *weights)


def _ssm_param_kernel(lre_ref, lim_ref, ldt_ref, bre_ref, bim_ref, are_ref, aim_ref, wb_ref):
    lre = lre_ref[...]
    lim = lim_ref[...]
    dt = jnp.exp(ldt_ref[...])
    mag = jnp.exp(lre * dt)
    abr = mag * jnp.cos(lim * dt)
    abi = mag * jnp.sin(lim * dt)
    are_ref[...] = abr
    aim_ref[...] = abi
    nr = abr - 1.0
    den = lre * lre + lim * lim
    cr = (nr * lre + abi * lim) / den
    ci = (abi * lre - nr * lim) / den
    for c in range(SSM_BLOCKS):
        sl = slice(c * BLOCK_STATES, (c + 1) * BLOCK_STATES)
        crc, cic = cr[:, sl], ci[:, sl]
        br, bi = bre_ref[c], bim_ref[c]
        wb_ref[c, :, :BLOCK_STATES] = (crc * br - cic * bi).astype(BF16)
        wb_ref[c, :, BLOCK_STATES:] = (crc * bi + cic * br).astype(BF16)


def _ssm_params(lre, lim, ldt, bre, bim):
    return pl.pallas_call(
        _ssm_param_kernel,
        out_shape=[jax.ShapeDtypeStruct((1, N_STATE), F32),
                   jax.ShapeDtypeStruct((1, N_STATE), F32),
                   jax.ShapeDtypeStruct((SSM_BLOCKS, LANES, 2 * BLOCK_STATES), BF16)],
        name="ssm_params",
    )(lre, lim, ldt, bre, bim)


def _load_history(tail_ref, cs0_ref, *, bb):
    for k in range(CONV_BLOCKS):
        tail_ref[k] = cs0_ref[:, :, k * LANES:(k + 1) * LANES].reshape(CONV_TAIL * bb, LANES)


def _advance_history(tail_ref, uc_ref, rows, *, bb):
    tail_rows = CONV_TAIL * bb
    if rows >= tail_rows:
        tail_ref[...] = uc_ref[:, rows - tail_rows:rows, :]
    else:
        tail_ref[:, 0:tail_rows - rows, :] = tail_ref[:, rows:tail_rows, :]
        tail_ref[:, tail_rows - rows:tail_rows, :] = uc_ref[...]


def _store_history(cs_ref, tail_ref, *, bb):
    for k in range(CONV_BLOCKS):
        cs_ref[:, :, k * LANES:(k + 1) * LANES] = tail_ref[k].reshape(CONV_TAIL, bb, LANES)


def _ssm_block(c, su, hbuf_ref, are_ref, aim_ref, wb_ref, hre_ref, him_ref, *, bb, steps):
    st = slice(c * BLOCK_STATES, (c + 1) * BLOCK_STATES)
    hbuf_ref[...] = jnp.dot(su.astype(BF16), wb_ref[c], preferred_element_type=F32)
    ar = jnp.broadcast_to(are_ref[:, st], (SUBLANES, BLOCK_STATES))
    ai = jnp.broadcast_to(aim_ref[:, st], (SUBLANES, BLOCK_STATES))
    for rb in range(bb // SUBLANES):
        seqs = slice(rb * SUBLANES, (rb + 1) * SUBLANES)
        hr, hi = hre_ref[seqs, st], him_ref[seqs, st]
        for t in range(steps):
            rows = slice(t * bb + rb * SUBLANES, t * bb + (rb + 1) * SUBLANES)
            nr = ar * hr - ai * hi + hbuf_ref[rows, :BLOCK_STATES]
            ni = ar * hi + ai * hr + hbuf_ref[rows, BLOCK_STATES:]
            hbuf_ref[rows, :BLOCK_STATES] = nr
            hbuf_ref[rows, BLOCK_STATES:] = ni
            hr, hi = nr, ni
        hre_ref[seqs, st] = hr
        him_ref[seqs, st] = hi


def _mixer_state_kernel(uc_ref, su_ref, cs0_ref, h0re_ref, h0im_ref, are_ref, aim_ref, wb_ref,
                        cs_ref, hre_ref, him_ref, tail_ref, hbuf_ref, *, bb, steps):
    _load_history(tail_ref, cs0_ref, bb=bb)
    _advance_history(tail_ref, uc_ref, steps * bb, bb=bb)
    _store_history(cs_ref, tail_ref, bb=bb)
    hre_ref[...] = h0re_ref[...]
    him_ref[...] = h0im_ref[...]
    for c in range(SSM_BLOCKS):
        _ssm_block(c, su_ref[c], hbuf_ref, are_ref, aim_ref, wb_ref, hre_ref, him_ref,
                   bb=bb, steps=steps)


def _mixer_state(uc, su, cs0, h0re, h0im, are, aim, wb):
    bb = cs0.shape[1]
    rows = uc.shape[1]
    return pl.pallas_call(
        functools.partial(_mixer_state_kernel, bb=bb, steps=rows // bb),
        out_shape=[jax.ShapeDtypeStruct((CONV_TAIL, bb, D_CONV), F32),
                   jax.ShapeDtypeStruct((bb, N_STATE), F32),
                   jax.ShapeDtypeStruct((bb, N_STATE), F32)],
        scratch_shapes=[pltpu.VMEM((CONV_BLOCKS, CONV_TAIL * bb, LANES), F32),
                        pltpu.VMEM((rows, 2 * BLOCK_STATES), F32)],
        name="mixer_state",
    )(uc, su, cs0, h0re, h0im, are, aim, wb)


def _mixer_ffn_kernel(uc_ref, su_ref, x1_ref, cs0_ref, h0re_ref, h0im_ref,
                      cw_ref, cb_ref, lg_ref, lb_ref, are_ref, aim_ref, wb_ref, wcre_ref, wcim_ref,
                      d_ref, gw_ref, gb_ref, wo_ref,
                      n2_ref, wg_ref, wu_ref, wd_ref, nf_ref,
                      y_ref, cs_ref, hre_ref, him_ref,
                      tail_ref, hbuf_ref, yc_ref, mix_ref, x2_ref, *state_scratch,
                      bb, steps, chunks, groups):
    rows = steps * bb
    tail_rows = CONV_TAIL * bb
    group, chunk = pl.program_id(0), pl.program_id(1)
    if state_scratch:
        all_re, all_im = state_scratch
        first = pl.multiple_of(group * bb, bb)
        state_re, state_im = all_re.at[pl.ds(first, bb), :], all_im.at[pl.ds(first, bb), :]

        @pl.when(jnp.logical_and(group == 0, chunk == 0))
        def _():
            all_re[...] = h0re_ref[...].T
            all_im[...] = h0im_ref[...].T
    else:
        state_re, state_im = hre_ref, him_ref

    @pl.when(chunk == 0)
    def _():
        if not state_scratch:
            hre_ref[...] = h0re_ref[...]
            him_ref[...] = h0im_ref[...]
        _load_history(tail_ref, cs0_ref, bb=bb)

    def history_or_chunk(k, row):
        if row < tail_rows:
            return tail_ref[k, row:row + SUBLANES, :]
        return uc_ref[k, row - tail_rows:row - tail_rows + SUBLANES, :]

    for r in range(rows // CONV_ROW_GROUP):
        o = r * CONV_ROW_GROUP
        for k in range(CONV_BLOCKS):
            cols = slice(k * LANES, (k + 1) * LANES)
            accs = [jnp.zeros((SUBLANES, LANES), F32)] * (CONV_ROW_GROUP // SUBLANES)
            for tap in range(CONV_K):
                wt = cw_ref[tap:tap + 1, cols]
                accs = [a + wt * history_or_chunk(k, o + q * SUBLANES + tap * bb)
                        for q, a in enumerate(accs)]
            for q, a in enumerate(accs):
                yc_ref[k, o + q * SUBLANES:o + (q + 1) * SUBLANES, :] = a + cb_ref[:, cols]

    _advance_history(tail_ref, uc_ref, rows, bb=bb)

    ycs = [yc_ref[k] for k in range(CONV_BLOCKS)]
    mu = sum(jnp.sum(y, axis=-1, keepdims=True) for y in ycs) * (1.0 / D_CONV)
    cens = [y - mu for y in ycs]
    var = sum(jnp.sum(c * c, axis=-1, keepdims=True) for c in cens) * (1.0 / D_CONV)
    inv = lax.rsqrt(var + LN_EPS)
    acts = []
    for k, cen in enumerate(cens):
        cols = slice(k * LANES, (k + 1) * LANES)
        yn = cen * inv * lg_ref[:, cols] + lb_ref[:, cols]
        acts.append((yn * jax.nn.sigmoid(yn)).astype(BF16))

    zs = []
    for c in range(SSM_BLOCKS):
        su = su_ref[c]
        hbuf = hbuf_ref.at[c % 2]
        _ssm_block(c, su, hbuf, are_ref, aim_ref, wb_ref, state_re, state_im, bb=bb, steps=steps)
        hb = hbuf[...].astype(BF16)
        y = (jnp.dot(hb[:, :BLOCK_STATES], wcre_ref[c], preferred_element_type=F32)
             - jnp.dot(hb[:, BLOCK_STATES:], wcim_ref[c], preferred_element_type=F32))
        zs.append(jax.nn.gelu(y + d_ref[:, c * LANES:(c + 1) * LANES] * su))

    z = jnp.concatenate(zs, axis=-1)
    gate = jnp.dot(z.astype(BF16), gw_ref[...], preferred_element_type=F32) + gb_ref[...]
    outs = (z * jax.nn.sigmoid(gate)).astype(BF16)

    mix = jnp.dot(jnp.concatenate(acts + [outs], axis=-1), wo_ref[...], preferred_element_type=F32)
    for k in range(MODEL_BLOCKS):
        mix_ref[k] = mix[:, k * LANES:(k + 1) * LANES]
    for b in range(bb):
        for k in range(MODEL_BLOCKS):
            cols = slice(k * LANES, (k + 1) * LANES)
            x2_ref[b * steps:(b + 1) * steps, cols] = (
                x1_ref[b, :, cols] + mix_ref[k, pl.ds(b, steps, stride=bb), :])

    x2 = x2_ref[...]
    hb2 = _rmsnorm(x2, n2_ref[...]).astype(BF16)
    x3 = x2 + 0.5 * _swiglu(hb2, wg_ref, wu_ref, wd_ref)
    y_ref[...] = _rmsnorm(x3, nf_ref[...]).reshape(bb, steps, D_MODEL)

    @pl.when(chunk == chunks - 1)
    def _():
        _store_history(cs_ref, tail_ref, bb=bb)

    if state_scratch:
        @pl.when(jnp.logical_and(group == groups - 1, chunk == chunks - 1))
        def _():
            hre_ref[...] = all_re[...].T
            him_ref[...] = all_im[...].T


def _mixer_ffn(uc, su, x1, cs0, h0re, h0im, consts, ffn, *, bb, steps):
    nseq, total, _ = x1.shape
    groups, chunks = nseq // bb, total // steps
    rows, tail_rows = steps * bb, CONV_TAIL * bb
    assert nseq % bb == 0 and total % steps == 0 and bb % SUBLANES == 0 and steps % SUBLANES == 0
    assert rows % CONV_ROW_GROUP == 0 and (groups == 1 or chunks == 1)
    state_major = h0re.shape == (N_STATE, nseq)
    assert state_major or h0re.shape == (nseq, N_STATE)
    seq_spec = pl.BlockSpec((bb, steps, D_MODEL), lambda g, i: (g, i, 0))
    mix_spec = pl.BlockSpec((CONV_BLOCKS, rows, LANES), lambda g, i: (0, g * chunks + i, 0))
    if state_major:
        state_spec = pl.BlockSpec((N_STATE, nseq), lambda g, i: (0, 0))
        state_scratch = [pltpu.VMEM((nseq, N_STATE), F32)] * 2
    else:
        state_spec = pl.BlockSpec((bb, N_STATE), lambda g, i: (g, 0))
        state_scratch = []
    cs_spec = pl.BlockSpec((CONV_TAIL, bb, D_CONV), lambda g, i: (0, g, 0))
    return pl.pallas_call(
        functools.partial(_mixer_ffn_kernel, bb=bb, steps=steps, chunks=chunks, groups=groups),
        grid=(groups, chunks),
        in_specs=[mix_spec, mix_spec, seq_spec, cs_spec, state_spec, state_spec]
                 + [_const_spec(c.shape) for c in consts + ffn],
        out_specs=[seq_spec, cs_spec, state_spec, state_spec],
        out_shape=[jax.ShapeDtypeStruct((nseq, total, D_MODEL), F32),
                   jax.ShapeDtypeStruct((CONV_TAIL, nseq, D_CONV), F32),
                   jax.ShapeDtypeStruct(h0re.shape, F32),
                   jax.ShapeDtypeStruct(h0re.shape, F32)],
        scratch_shapes=[pltpu.VMEM((CONV_BLOCKS, tail_rows, LANES), F32),
                        pltpu.VMEM((2, rows, 2 * BLOCK_STATES), F32),
                        pltpu.VMEM((CONV_BLOCKS, rows, LANES), F32),
                        pltpu.VMEM((MODEL_BLOCKS, rows, LANES), F32),
                        pltpu.VMEM((rows, D_MODEL), F32)] + state_scratch,
        compiler_params=pltpu.CompilerParams(dimension_semantics=("arbitrary", "arbitrary"),
                                             vmem_limit_bytes=VMEM_LIMIT_BYTES),
        name="mixer_ffn",
    )(uc, su, x1, cs0, h0re, h0im, *consts, *ffn)


def _block_diag(blocks):
    nb, n, a, b = blocks.shape
    eye = jnp.eye(n, dtype=blocks.dtype)
    return jnp.einsum("cgab,gh->cgahb", blocks, eye).reshape(nb, n * a, n * b)


PROMPT_STEPS = 64
PROMPT_FFN_STEPS = 128
SAMPLE_SEQS = 32
SAMPLE_FFN_SEQS = 64


def kernel(x_prompt, x_sample, state_conv, state_ssm_re, state_ssm_im, meta_tokens, ffn1_norm, ffn1_w_gate, ffn1_w_up, ffn1_w_down, mix_norm, w_in, conv_w, conv_b, conv_ln_g, conv_ln_b, ssm_lambda_re, ssm_lambda_im, ssm_log_dt, ssm_b_re, ssm_b_im, ssm_c_re, ssm_c_im, ssm_d, ssm_glu_w, ssm_glu_b, w_out, ffn2_norm, ffn2_w_gate, ffn2_w_up, ffn2_w_down, final_norm):
    assert ffn1_norm.shape[0] == 1, "one layer"
    bp, lp, _ = x_prompt.shape
    bs, ls, _ = x_sample.shape
    groups_per_block = SSM_GROUPS // SSM_BLOCKS

    row = lambda v: v.reshape(1, -1).astype(F32)
    w1 = (row(ffn1_norm[0]), ffn1_w_gate[0].astype(BF16), ffn1_w_up[0].astype(BF16),
          ffn1_w_down[0].astype(BF16), row(mix_norm[0]), w_in[0].astype(BF16))
    b_blocks = lambda b: _block_diag(
        b.reshape(SSM_BLOCKS, groups_per_block, SSM_P, SSM_GC).transpose(0, 1, 3, 2))
    c_blocks = lambda c: _block_diag(
        c.reshape(SSM_BLOCKS, groups_per_block, SSM_GC, SSM_P).transpose(0, 1, 3, 2)).astype(BF16)
    are, aim, wb = _ssm_params(row(ssm_lambda_re[0]), row(ssm_lambda_im[0]),
                               row(jnp.repeat(ssm_log_dt[0], SSM_P)),
                               b_blocks(ssm_b_re[0]), b_blocks(ssm_b_im[0]))

    meta = jnp.broadcast_to(meta_tokens.astype(x_prompt.dtype)[None], (bp, N_META, D_MODEL))
    x1p, ucp, sup, ucm, sum_, w2g, w2u, w2d, wo, gw = _ffn_in_prefixed(
        x_prompt, meta, w1, (ffn2_w_gate[0], ffn2_w_up[0], ffn2_w_down[0], w_out[0], ssm_glu_w[0]),
        steps=PROMPT_FFN_STEPS)
    w2 = (row(ffn2_norm[0]), w2g, w2u, w2d, row(final_norm))
    consts = (conv_w[0], row(conv_b[0]), row(conv_ln_g[0]), row(conv_ln_b[0]), are, aim, wb,
              c_blocks(ssm_c_re[0]), c_blocks(ssm_c_im[0]), row(ssm_d[0]),
              gw, row(ssm_glu_b[0]), wo)
    cs_m, hre_m, him_m = _mixer_state(
        ucm, sum_, jnp.zeros((CONV_TAIL, bp, D_CONV), F32), jnp.zeros((bp, N_STATE), F32),
        jnp.zeros((bp, N_STATE), F32), are, aim, wb)
    y_prompt, cs_p, hre_p, him_p = _mixer_ffn(ucp, sup, x1p, cs_m, hre_m, him_m, consts, w2,
                                              bb=bp, steps=PROMPT_STEPS)

    x1s, ucs, sus = _ffn_in(x_sample, w1, seqs=SAMPLE_FFN_SEQS, bb=SAMPLE_SEQS)
    state_in = lambda h: h[0].transpose(1, 2, 0).reshape(N_STATE, bs)
    state_out = lambda h: h.reshape(SSM_GROUPS, SSM_P, bs).transpose(2, 0, 1)[None]
    y_sample, cs_s, hre_s, him_s = _mixer_ffn(
        ucs, sus, x1s, state_conv[0].transpose(1, 0, 2), state_in(state_ssm_re),
        state_in(state_ssm_im), consts, w2, bb=SAMPLE_SEQS, steps=ls)

    state = lambda h, b: h.reshape(1, b, SSM_GROUPS, SSM_P)
    history = lambda cs: cs.transpose(1, 0, 2)[None]
    return (y_prompt, y_sample, history(cs_p), state(hre_p, bp), state(him_p, bp),
            history(cs_s), state_out(hre_s), state_out(him_s))
```

```python
import functools

import jax
import jax.numpy as jnp
from jax import lax
from jax.experimental import pallas as pl
from jax.experimental.pallas import tpu as pltpu

F32 = jnp.float32
BF16 = jnp.bfloat16

D_MODEL = 1024
D_FF = 2816
D_CONV = 512
D_SSM = 512
CONV_K = 31
CONV_TAIL = CONV_K - 1
N_META = 16
SSM_GROUPS = 32
SSM_GC = 16
SSM_P = 64
N_STATE = SSM_GROUPS * SSM_P
RMS_EPS = 1e-6
LN_EPS = 1e-5

SUBLANES = 8
LANES = 128
FF_CHUNK = 256
N_FF_CHUNKS = D_FF // FF_CHUNK
CONV_BLOCKS = D_CONV // LANES
SSM_BLOCKS = D_SSM // LANES
MODEL_BLOCKS = D_MODEL // LANES
BLOCK_STATES = N_STATE // SSM_BLOCKS
CONV_ROW_GROUP = 8 * SUBLANES
SCAN_PIECE_TILES = 8
VMEM_LIMIT_BYTES = 60 * 1024 * 1024


def _const_spec(shape):
    zeros = (0,) * len(shape)
    return pl.BlockSpec(shape, lambda *_: zeros, pipeline_mode=pl.Buffered(1))


def _rmsnorm(x, g):
    return x * lax.rsqrt(jnp.mean(x * x, axis=-1, keepdims=True) + RMS_EPS) * g


def _swiglu(hb, wg_ref, wu_ref, wd_ref):
    acc = jnp.zeros((hb.shape[0], D_MODEL), F32)
    for c in range(N_FF_CHUNKS):
        cols = slice(c * FF_CHUNK, (c + 1) * FF_CHUNK)
        g = jnp.dot(hb, wg_ref[:, cols], preferred_element_type=F32)
        u = jnp.dot(hb, wu_ref[:, cols], preferred_element_type=F32)
        a = (g * jax.nn.sigmoid(g) * u).astype(BF16)
        acc = acc + jnp.dot(a, wd_ref[cols, :], preferred_element_type=F32)
    return acc


def _store_step_major(dst_ref, val, *, nseq, steps, bb):
    for s in range(nseq):
        base = (s // bb) * (bb * steps) + s % bb
        for k in range(val.shape[-1] // LANES):
            dst_ref[k, pl.ds(base, steps, stride=bb), :] = (
                val[s * steps:(s + 1) * steps, k * LANES:(k + 1) * LANES])


def _ffn_in_rows(x_ref, n1_ref, wg_ref, wu_ref, wd_ref, n2_ref, win_ref, x1_ref, uc_ref, su_ref, *, bb):
    nseq, steps, _ = x_ref.shape
    x = x_ref[...].reshape(nseq * steps, D_MODEL)
    hb = _rmsnorm(x, n1_ref[...]).astype(BF16)
    x1 = x + 0.5 * _swiglu(hb, wg_ref, wu_ref, wd_ref)
    if x1_ref is not None:
        x1_ref[...] = x1.reshape(nseq, steps, D_MODEL)
    h2 = _rmsnorm(x1, n2_ref[...]).astype(BF16)
    proj = jnp.dot(h2, win_ref[...], preferred_element_type=F32)
    cv = proj[:, :D_CONV]
    cg = proj[:, D_CONV:2 * D_CONV]
    _store_step_major(uc_ref, cv * jax.nn.sigmoid(cg), nseq=nseq, steps=steps, bb=bb)
    _store_step_major(su_ref, proj[:, 2 * D_CONV:], nseq=nseq, steps=steps, bb=bb)


def _ffn_in_kernel(x_ref, *refs, bb):
    _ffn_in_rows(x_ref, *refs, bb=bb)


def _ffn_in(x, weights, *, seqs, bb):
    nseq, steps, _ = x.shape
    assert nseq % seqs == 0 and seqs % bb == 0
    rows = seqs * steps
    mix_spec = pl.BlockSpec((CONV_BLOCKS, rows, LANES), lambda i: (0, i, 0))
    mix_shape = jax.ShapeDtypeStruct((CONV_BLOCKS, nseq * steps, LANES), F32)
    return pl.pallas_call(
        functools.partial(_ffn_in_kernel, bb=bb),
        grid=(nseq // seqs,),
        in_specs=[pl.BlockSpec((seqs, steps, D_MODEL), lambda i: (i, 0, 0))]
                 + [_const_spec(w.shape) for w in weights],
        out_specs=[pl.BlockSpec((seqs, steps, D_MODEL), lambda i: (i, 0, 0)), mix_spec, mix_spec],
        out_shape=[jax.ShapeDtypeStruct((nseq, steps, D_MODEL), F32), mix_shape, mix_shape],
        compiler_params=pltpu.CompilerParams(dimension_semantics=("parallel",),
                                             vmem_limit_bytes=VMEM_LIMIT_BYTES),
        name="ffn_in",
    )(x, *weights)


def _ffn_in_prefixed_kernel(*refs, tiles, n_cast, bb):
    x_ref, pre_ref = refs[:2]
    cast_in = refs[2:2 + n_cast]
    weights = refs[2 + n_cast:8 + n_cast]
    x1_ref, uc_ref, su_ref, ucp_ref, sup_ref = refs[8 + n_cast:13 + n_cast]
    cast_out = refs[13 + n_cast:]
    i = pl.program_id(0)

    @pl.when(i < tiles)
    def _():
        for src, dst in zip(cast_in, cast_out):
            dst[...] = src[...].astype(BF16)
        _ffn_in_rows(x_ref, *weights, x1_ref, uc_ref, su_ref, bb=bb)

    @pl.when(i == tiles)
    def _():
        _ffn_in_rows(pre_ref, *weights, None, ucp_ref, sup_ref, bb=bb)


def _ffn_in_prefixed(x, prefix, weights, to_cast, *, steps):
    nseq, total, _ = x.shape
    pre_steps = prefix.shape[1]
    assert total % steps == 0
    tiles = total // steps
    clamp = lambda i: jnp.minimum(i, tiles - 1)
    mix_spec = pl.BlockSpec((CONV_BLOCKS, nseq * steps, LANES), lambda i: (0, clamp(i), 0))
    mix_shape = jax.ShapeDtypeStruct((CONV_BLOCKS, nseq * total, LANES), F32)
    pre_spec = pl.BlockSpec((CONV_BLOCKS, nseq * pre_steps, LANES), lambda i: (0, 0, 0))
    pre_shape = jax.ShapeDtypeStruct((CONV_BLOCKS, nseq * pre_steps, LANES), F32)
    slab_specs = []
    for w in to_cast:
        assert w.shape[0] % (tiles * 2 * SUBLANES) == 0, w.shape
        slab_specs.append(pl.BlockSpec((w.shape[0] // tiles, w.shape[1]), lambda i: (clamp(i), 0)))
    return pl.pallas_call(
        functools.partial(_ffn_in_prefixed_kernel, tiles=tiles, n_cast=len(to_cast), bb=nseq),
        grid=(tiles + 1,),
        in_specs=[pl.BlockSpec((nseq, steps, D_MODEL), lambda i: (0, clamp(i), 0)),
                  pl.BlockSpec(prefix.shape, lambda i: (0, 0, 0))] + slab_specs
                 + [_const_spec(w.shape) for w in weights],
        out_specs=[pl.BlockSpec((nseq, steps, D_MODEL), lambda i: (0, clamp(i), 0)),
                   mix_spec, mix_spec, pre_spec, pre_spec] + slab_specs,
        out_shape=[jax.ShapeDtypeStruct((nseq, total, D_MODEL), F32), mix_shape, mix_shape,
                   pre_shape, pre_shape] + [jax.ShapeDtypeStruct(w.shape, BF16) for w in to_cast],
        compiler_params=pltpu.CompilerParams(dimension_semantics=("arbitrary",),
                                             vmem_limit_bytes=VMEM_LIMIT_BYTES),
        name="ffn_in_prefixed",
    )(x, prefix, *to_cast, *weights)


def _ssm_param_kernel(lre_ref, lim_ref, ldt_ref, bre_ref, bim_ref, are_ref, aim_ref, wb_ref):
    lre = lre_ref[...]
    lim = lim_ref[...]
    dt = jnp.exp(ldt_ref[...])
    mag = jnp.exp(lre * dt)
    abr = mag * jnp.cos(lim * dt)
    abi = mag * jnp.sin(lim * dt)
    are_ref[...] = abr
    aim_ref[...] = abi
    nr = abr - 1.0
    den = lre * lre + lim * lim
    cr = (nr * lre + abi * lim) / den
    ci = (abi * lre - nr * lim) / den
    for c in range(SSM_BLOCKS):
        sl = slice(c * BLOCK_STATES, (c + 1) * BLOCK_STATES)
        crc, cic = cr[:, sl], ci[:, sl]
        br, bi = bre_ref[c], bim_ref[c]
        wb_ref[c, :, :BLOCK_STATES] = (crc * br - cic * bi).astype(BF16)
        wb_ref[c, :, BLOCK_STATES:] = (crc * bi + cic * br).astype(BF16)


def _ssm_params(lre, lim, ldt, bre, bim):
    return pl.pallas_call(
        _ssm_param_kernel,
        out_shape=[jax.ShapeDtypeStruct((1, N_STATE), F32),
                   jax.ShapeDtypeStruct((1, N_STATE), F32),
                   jax.ShapeDtypeStruct((SSM_BLOCKS, LANES, 2 * BLOCK_STATES), BF16)],
        name="ssm_params",
    )(lre, lim, ldt, bre, bim)


def _load_history(tail_ref, cs0_ref, *, bb):
    for k in range(CONV_BLOCKS):
        tail_ref[k] = cs0_ref[:, :, k * LANES:(k + 1) * LANES].reshape(CONV_TAIL * bb, LANES)


def _advance_history(tail_ref, uc_ref, rows, *, bb):
    tail_rows = CONV_TAIL * bb
    if rows >= tail_rows:
        tail_ref[...] = uc_ref[:, rows - tail_rows:rows, :]
    else:
        tail_ref[:, 0:tail_rows - rows, :] = tail_ref[:, rows:tail_rows, :]
        tail_ref[:, tail_rows - rows:tail_rows, :] = uc_ref[...]


def _store_history(cs_ref, tail_ref, *, bb):
    for k in range(CONV_BLOCKS):
        cs_ref[:, :, k * LANES:(k + 1) * LANES] = tail_ref[k].reshape(CONV_TAIL, bb, LANES)


def _zero_after(tile):
    bits = lax.shift_right_logical(lax.shift_right_logical(tile.astype(jnp.int32), 16), 16)
    return bits.astype(F32)


def _ordered_after(x, zero):
    return x + jnp.concatenate([zero] * (x.shape[-1] // LANES), axis=-1)


def _project(c, su, hbuf_ref, wb_ref):
    hbuf_ref[...] = jnp.dot(su.astype(BF16), wb_ref[c], preferred_element_type=F32)
    return hbuf_ref[0:SUBLANES, 0:LANES]


def _ssm_block(c, su, hbuf_ref, hout_ref, are_ref, aim_ref, wb_ref, hre_ref, him_ref, *, bb, steps,
               side_work=()):
    st = slice(c * BLOCK_STATES, (c + 1) * BLOCK_STATES)
    if su is not None:
        _project(c, su, hbuf_ref, wb_ref)
    ar = jnp.broadcast_to(are_ref[:, st], (SUBLANES, BLOCK_STATES))
    ai = jnp.broadcast_to(aim_ref[:, st], (SUBLANES, BLOCK_STATES))
    seq_blocks = bb // SUBLANES
    seqs = [slice(rb * SUBLANES, (rb + 1) * SUBLANES) for rb in range(seq_blocks)]
    h = [(hre_ref[s, st], him_ref[s, st]) for s in seqs]
    side_work = list(side_work)
    pending = None
    for tile in range(steps * seq_blocks):
        rb = tile % seq_blocks
        rows = slice(tile * SUBLANES, (tile + 1) * SUBLANES)
        hr, hi = h[rb]
        if side_work and tile % SCAN_PIECE_TILES == 0:
            hr = _ordered_after(hr, _zero_after(side_work.pop(0)()))
        nr = ar * hr - ai * hi + hbuf_ref[rows, :BLOCK_STATES]
        ni = ar * hi + ai * hr + hbuf_ref[rows, BLOCK_STATES:]
        h[rb] = (nr, ni)
        if hout_ref is None:
            continue
        if pending is None:
            pending = (nr, ni)
        else:
            pair = slice((tile - 1) * SUBLANES, (tile + 1) * SUBLANES)
            hout_ref[pair, :BLOCK_STATES] = jnp.concatenate([pending[0], nr], axis=0).astype(BF16)
            hout_ref[pair, BLOCK_STATES:] = jnp.concatenate([pending[1], ni], axis=0).astype(BF16)
            pending = None
    for work in side_work:
        work()
    for s, (hr, hi) in zip(seqs, h):
        hre_ref[s, st] = hr
        him_ref[s, st] = hi


def _mixer_state_kernel(uc_ref, su_ref, cs0_ref, h0re_ref, h0im_ref, are_ref, aim_ref, wb_ref,
                        cs_ref, hre_ref, him_ref, tail_ref, hbuf_ref, *, bb, steps):
    _load_history(tail_ref, cs0_ref, bb=bb)
    _advance_history(tail_ref, uc_ref, steps * bb, bb=bb)
    _store_history(cs_ref, tail_ref, bb=bb)
    hre_ref[...] = h0re_ref[...]
    him_ref[...] = h0im_ref[...]
    for c in range(SSM_BLOCKS):
        _ssm_block(c, su_ref[c], hbuf_ref, None, are_ref, aim_ref, wb_ref, hre_ref, him_ref,
                   bb=bb, steps=steps)


def _mixer_state(uc, su, cs0, h0re, h0im, are, aim, wb):
    bb = cs0.shape[1]
    rows = uc.shape[1]
    return pl.pallas_call(
        functools.partial(_mixer_state_kernel, bb=bb, steps=rows // bb),
        out_shape=[jax.ShapeDtypeStruct((CONV_TAIL, bb, D_CONV), F32),
                   jax.ShapeDtypeStruct((bb, N_STATE), F32),
                   jax.ShapeDtypeStruct((bb, N_STATE), F32)],
        scratch_shapes=[pltpu.VMEM((CONV_BLOCKS, CONV_TAIL * bb, LANES), F32),
                        pltpu.VMEM((rows, 2 * BLOCK_STATES), F32)],
        name="mixer_state",
    )(uc, su, cs0, h0re, h0im, are, aim, wb)


N_MIXER_INPUTS = 24


def _mixer_ffn_kernel(*refs, subs, bb, steps, chunks, groups):
    group, step = pl.program_id(0), pl.program_id(1)
    chunk_fn = functools.partial(_mixer_ffn_chunk, bb=bb, steps=steps, chunks=chunks, groups=groups,
                                 group=group)
    if subs == 1:
        chunk_fn(*refs, chunk=step)
        return
    rows = steps * bb

    def sub(s, carry):
        row0 = pl.multiple_of(s * rows, rows)
        step0 = pl.multiple_of(s * steps, steps)
        view = list(refs)
        for i in (0, 1):
            view[i] = refs[i].at[:, pl.ds(row0, rows), :]
        for i in (2, N_MIXER_INPUTS):
            view[i] = refs[i].at[:, pl.ds(step0, steps), :]
        chunk_fn(*view, chunk=step * subs + s)
        return carry

    lax.fori_loop(0, subs, sub, 0)


def _mixer_ffn_chunk(uc_ref, su_ref, x1_ref, cs0_ref, h0re_ref, h0im_ref,
                     cw_ref, cb_ref, lg_ref, lb_ref, are_ref, aim_ref, wb_ref, wcre_ref, wcim_ref,
                     d_ref, gw_ref, gb_ref, wo_ref,
                     n2_ref, wg_ref, wu_ref, wd_ref, nf_ref,
                     y_ref, cs_ref, hre_ref, him_ref,
                     tail_ref, hbuf_ref, hout_ref, yc_ref, mix_ref, x2_ref, *state_scratch,
                     bb, steps, chunks, groups, group, chunk):
    rows = steps * bb
    tail_rows = CONV_TAIL * bb
    if state_scratch:
        all_re, all_im = state_scratch
        first = pl.multiple_of(group * bb, bb)
        state_re, state_im = all_re.at[pl.ds(first, bb), :], all_im.at[pl.ds(first, bb), :]

        @pl.when(jnp.logical_and(group == 0, chunk == 0))
        def _():
            all_re[...] = h0re_ref[...].T
            all_im[...] = h0im_ref[...].T
    else:
        state_re, state_im = hre_ref, him_ref

    @pl.when(chunk == 0)
    def _():
        if not state_scratch:
            hre_ref[...] = h0re_ref[...]
            him_ref[...] = h0im_ref[...]
        _load_history(tail_ref, cs0_ref, bb=bb)

    def history_or_chunk(k, row):
        if row < tail_rows:
            return tail_ref[k, row:row + SUBLANES, :]
        return uc_ref[k, row - tail_rows:row - tail_rows + SUBLANES, :]

    def conv_unit(r, k):
        o = r * CONV_ROW_GROUP
        cols = slice(k * LANES, (k + 1) * LANES)
        accs = [jnp.zeros((SUBLANES, LANES), F32)] * (CONV_ROW_GROUP // SUBLANES)
        for tap in range(CONV_K):
            wt = cw_ref[tap:tap + 1, cols]
            accs = [a + wt * history_or_chunk(k, o + q * SUBLANES + tap * bb)
                    for q, a in enumerate(accs)]
        for q, a in enumerate(accs):
            yc_ref[k, o + q * SUBLANES:o + (q + 1) * SUBLANES, :] = a + cb_ref[:, cols]
        return functools.reduce(jnp.add, accs)

    conv = [functools.partial(conv_unit, r, k)
            for r in range(rows // CONV_ROW_GROUP) for k in range(CONV_BLOCKS)]
    per_block = len(conv) // SSM_BLOCKS

    def with_projection(work, c):
        return lambda: work() + _project(c, su_ref[c], hbuf_ref.at[c % 2], wb_ref)

    zs = []
    def ssm_out(c):
        hb = hout_ref[c % 2]
        y = (jnp.dot(hb[:, :BLOCK_STATES], wcre_ref[c], preferred_element_type=F32)
             - jnp.dot(hb[:, BLOCK_STATES:], wcim_ref[c], preferred_element_type=F32))
        zs.append(jax.nn.gelu(y + d_ref[:, c * LANES:(c + 1) * LANES] * su_ref[c]))
        return zs[-1][0:SUBLANES]

    def with_output(work, c):
        return lambda: work() + ssm_out(c)

    _project(0, su_ref[0], hbuf_ref.at[0], wb_ref)
    for c in range(SSM_BLOCKS):
        work = conv[c * per_block:(c + 1) * per_block]
        if c > 0:
            work[1] = with_output(work[1], c - 1)
        if c + 1 < SSM_BLOCKS:
            work[per_block // 2] = with_projection(work[per_block // 2], c + 1)
        _ssm_block(c, None, hbuf_ref.at[c % 2], hout_ref.at[c % 2], are_ref, aim_ref, wb_ref,
                   state_re, state_im, bb=bb, steps=steps, side_work=work)
    ssm_out(SSM_BLOCKS - 1)

    _advance_history(tail_ref, uc_ref, rows, bb=bb)

    ycs = [yc_ref[k] for k in range(CONV_BLOCKS)]
    mu = sum(jnp.sum(y, axis=-1, keepdims=True) for y in ycs) * (1.0 / D_CONV)
    cens = [y - mu for y in ycs]
    var = sum(jnp.sum(c * c, axis=-1, keepdims=True) for c in cens) * (1.0 / D_CONV)
    inv = lax.rsqrt(var + LN_EPS)
    acts = []
    for k, cen in enumerate(cens):
        cols = slice(k * LANES, (k + 1) * LANES)
        yn = cen * inv * lg_ref[:, cols] + lb_ref[:, cols]
        acts.append((yn * jax.nn.sigmoid(yn)).astype(BF16))

    z = jnp.concatenate(zs, axis=-1)
    gate = jnp.dot(z.astype(BF16), gw_ref[...], preferred_element_type=F32) + gb_ref[...]
    outs = (z * jax.nn.sigmoid(gate)).astype(BF16)

    mix = jnp.dot(jnp.concatenate(acts + [outs], axis=-1), wo_ref[...], preferred_element_type=F32)
    for k in range(MODEL_BLOCKS):
        mix_ref[k] = mix[:, k * LANES:(k + 1) * LANES]
    for b in range(bb):
        for k in range(MODEL_BLOCKS):
            cols = slice(k * LANES, (k + 1) * LANES)
            x2_ref[b * steps:(b + 1) * steps, cols] = (
                x1_ref[b, :, cols] + mix_ref[k, pl.ds(b, steps, stride=bb), :])

    x2 = x2_ref[...]
    hb2 = _rmsnorm(x2, n2_ref[...]).astype(BF16)
    x3 = x2 + 0.5 * _swiglu(hb2, wg_ref, wu_ref, wd_ref)
    y_ref[...] = _rmsnorm(x3, nf_ref[...]).reshape(bb, steps, D_MODEL)

    @pl.when(chunk == chunks - 1)
    def _():
        _store_history(cs_ref, tail_ref, bb=bb)

    if state_scratch:
        @pl.when(jnp.logical_and(group == groups - 1, chunk == chunks - 1))
        def _():
            hre_ref[...] = all_re[...].T
            him_ref[...] = all_im[...].T


def _mixer_ffn(uc, su, x1, cs0, h0re, h0im, consts, ffn, *, bb, steps, subs=1):
    nseq, total, _ = x1.shape
    groups, chunks = nseq // bb, total // steps
    rows, tail_rows = steps * bb, CONV_TAIL * bb
    assert nseq % bb == 0 and total % steps == 0 and bb % SUBLANES == 0 and steps % SUBLANES == 0
    assert rows % CONV_ROW_GROUP == 0 and (groups == 1 or chunks == 1) and chunks % subs == 0
    state_major = h0re.shape == (N_STATE, nseq)
    assert state_major or h0re.shape == (nseq, N_STATE)
    grid_chunks = chunks // subs
    seq_spec = pl.BlockSpec((bb, steps * subs, D_MODEL), lambda g, i: (g, i, 0))
    mix_spec = pl.BlockSpec((CONV_BLOCKS, rows * subs, LANES), lambda g, i: (0, g * grid_chunks + i, 0))
    if state_major:
        state_spec = pl.BlockSpec((N_STATE, nseq), lambda g, i: (0, 0))
        state_scratch = [pltpu.VMEM((nseq, N_STATE), F32)] * 2
    else:
        state_spec = pl.BlockSpec((bb, N_STATE), lambda g, i: (g, 0))
        state_scratch = []
    cs_spec = pl.BlockSpec((CONV_TAIL, bb, D_CONV), lambda g, i: (0, g, 0))
    return pl.pallas_call(
        functools.partial(_mixer_ffn_kernel, subs=subs, bb=bb, steps=steps, chunks=chunks,
                          groups=groups),
        grid=(groups, grid_chunks),
        in_specs=[mix_spec, mix_spec, seq_spec, cs_spec, state_spec, state_spec]
                 + [_const_spec(c.shape) for c in consts + ffn],
        out_specs=[seq_spec, cs_spec, state_spec, state_spec],
        out_shape=[jax.ShapeDtypeStruct((nseq, total, D_MODEL), F32),
                   jax.ShapeDtypeStruct((CONV_TAIL, nseq, D_CONV), F32),
                   jax.ShapeDtypeStruct(h0re.shape, F32),
                   jax.ShapeDtypeStruct(h0re.shape, F32)],
        scratch_shapes=[pltpu.VMEM((CONV_BLOCKS, tail_rows, LANES), F32),
                        pltpu.VMEM((2, rows, 2 * BLOCK_STATES), F32),
                        pltpu.VMEM((2, rows, 2 * BLOCK_STATES), BF16),
                        pltpu.VMEM((CONV_BLOCKS, rows, LANES), F32),
                        pltpu.VMEM((MODEL_BLOCKS, rows, LANES), F32),
                        pltpu.VMEM((rows, D_MODEL), F32)] + state_scratch,
        compiler_params=pltpu.CompilerParams(dimension_semantics=("arbitrary", "arbitrary"),
                                             vmem_limit_bytes=VMEM_LIMIT_BYTES),
        name="mixer_ffn",
    )(uc, su, x1, cs0, h0re, h0im, *consts, *ffn)


def _block_diag(blocks):
    nb, n, a, b = blocks.shape
    eye = jnp.eye(n, dtype=blocks.dtype)
    return jnp.einsum("cgab,gh->cgahb", blocks, eye).reshape(nb, n * a, n * b)


PROMPT_STEPS = 64
PROMPT_SUBS = 2
PROMPT_FFN_STEPS = 128
SAMPLE_SEQS = 32
SAMPLE_FFN_SEQS = 64


def kernel(x_prompt, x_sample, state_conv, state_ssm_re, state_ssm_im, meta_tokens, ffn1_norm, ffn1_w_gate, ffn1_w_up, ffn1_w_down, mix_norm, w_in, conv_w, conv_b, conv_ln_g, conv_ln_b, ssm_lambda_re, ssm_lambda_im, ssm_log_dt, ssm_b_re, ssm_b_im, ssm_c_re, ssm_c_im, ssm_d, ssm_glu_w, ssm_glu_b, w_out, ffn2_norm, ffn2_w_gate, ffn2_w_up, ffn2_w_down, final_norm):
    assert ffn1_norm.shape[0] == 1, "one layer"
    bp, lp, _ = x_prompt.shape
    bs, ls, _ = x_sample.shape
    groups_per_block = SSM_GROUPS // SSM_BLOCKS

    row = lambda v: v.reshape(1, -1).astype(F32)
    w1 = (row(ffn1_norm[0]), ffn1_w_gate[0].astype(BF16), ffn1_w_up[0].astype(BF16),
          ffn1_w_down[0].astype(BF16), row(mix_norm[0]), w_in[0].astype(BF16))
    b_blocks = lambda b: _block_diag(
        b.reshape(SSM_BLOCKS, groups_per_block, SSM_P, SSM_GC).transpose(0, 1, 3, 2))
    c_blocks = lambda c: _block_diag(
        c.reshape(SSM_BLOCKS, groups_per_block, SSM_GC, SSM_P).transpose(0, 1, 3, 2)).astype(BF16)
    are, aim, wb = _ssm_params(row(ssm_lambda_re[0]), row(ssm_lambda_im[0]),
                               row(jnp.repeat(ssm_log_dt[0], SSM_P)),
                               b_blocks(ssm_b_re[0]), b_blocks(ssm_b_im[0]))

    meta = jnp.broadcast_to(meta_tokens.astype(x_prompt.dtype)[None], (bp, N_META, D_MODEL))
    x1p, ucp, sup, ucm, sum_, w2g, w2u, w2d, wo, gw = _ffn_in_prefixed(
        x_prompt, meta, w1, (ffn2_w_gate[0], ffn2_w_up[0], ffn2_w_down[0], w_out[0], ssm_glu_w[0]),
        steps=PROMPT_FFN_STEPS)
    w2 = (row(ffn2_norm[0]), w2g, w2u, w2d, row(final_norm))
    consts = (conv_w[0], row(conv_b[0]), row(conv_ln_g[0]), row(conv_ln_b[0]), are, aim, wb,
              c_blocks(ssm_c_re[0]), c_blocks(ssm_c_im[0]), row(ssm_d[0]),
              gw, row(ssm_glu_b[0]), wo)
    cs_m, hre_m, him_m = _mixer_state(
        ucm, sum_, jnp.zeros((CONV_TAIL, bp, D_CONV), F32), jnp.zeros((bp, N_STATE), F32),
        jnp.zeros((bp, N_STATE), F32), are, aim, wb)
    y_prompt, cs_p, hre_p, him_p = _mixer_ffn(ucp, sup, x1p, cs_m, hre_m, him_m, consts, w2,
                                              bb=bp, steps=PROMPT_STEPS, subs=PROMPT_SUBS)

    x1s, ucs, sus = _ffn_in(x_sample, w1, seqs=SAMPLE_FFN_SEQS, bb=SAMPLE_SEQS)
    state_in = lambda h: h[0].transpose(1, 2, 0).reshape(N_STATE, bs)
    state_out = lambda h: h.reshape(SSM_GROUPS, SSM_P, bs).transpose(2, 0, 1)[None]
    y_sample, cs_s, hre_s, him_s = _mixer_ffn(
        ucs, sus, x1s, state_conv[0].transpose(1, 0, 2), state_in(state_ssm_re),
        state_in(state_ssm_im), consts, w2, bb=SAMPLE_SEQS, steps=ls)

    state = lambda h, b: h.reshape(1, b, SSM_GROUPS, SSM_P)
    history = lambda cs: cs.transpose(1, 0, 2)[None]
    return (y_prompt, y_sample, history(cs_p), state(hre_p, bp), state(him_p, bp),
            history(cs_s), state_out(hre_s), state_out(him_s))
```

```python
import functools

import jax
import jax.numpy as jnp
from jax import lax
from jax.experimental import pallas as pl
from jax.experimental.pallas import tpu as pltpu

F32 = jnp.float32
BF16 = jnp.bfloat16

D_MODEL = 1024
D_FF = 2816
D_CONV = 512
D_SSM = 512
CONV_K = 31
CONV_TAIL = CONV_K - 1
N_META = 16
SSM_GROUPS = 32
SSM_GC = 16
SSM_P = 64
N_STATE = SSM_GROUPS * SSM_P
RMS_EPS = 1e-6
LN_EPS = 1e-5

SUBLANES = 8
LANES = 128
FF_CHUNK = 256
N_FF_CHUNKS = D_FF // FF_CHUNK
CONV_BLOCKS = D_CONV // LANES
SSM_BLOCKS = D_SSM // LANES
MODEL_BLOCKS = D_MODEL // LANES
BLOCK_STATES = N_STATE // SSM_BLOCKS
CONV_ROW_GROUP = 8 * SUBLANES
SCAN_PIECE_TILES = 8
VMEM_LIMIT_BYTES = 60 * 1024 * 1024


def _const_spec(shape):
    zeros = (0,) * len(shape)
    return pl.BlockSpec(shape, lambda *_: zeros, pipeline_mode=pl.Buffered(1))


def _rmsnorm(x, g):
    return x * lax.rsqrt(jnp.mean(x * x, axis=-1, keepdims=True) + RMS_EPS) * g


def _rms_split(x, g):
    return (x * g).astype(BF16), lax.rsqrt(jnp.mean(x * x, axis=-1, keepdims=True) + RMS_EPS)


def _swiglu(hb, inv_rms, wg_ref, wu_ref, wd_ref):
    acc = jnp.zeros((hb.shape[0], D_MODEL), F32)
    for c in range(N_FF_CHUNKS):
        cols = slice(c * FF_CHUNK, (c + 1) * FF_CHUNK)
        g = jnp.dot(hb, wg_ref[:, cols], preferred_element_type=F32) * inv_rms
        u = jnp.dot(hb, wu_ref[:, cols], preferred_element_type=F32) * inv_rms
        a = (g * jax.nn.sigmoid(g) * u).astype(BF16)
        acc = acc + jnp.dot(a, wd_ref[cols, :], preferred_element_type=F32)
    return acc


def _store_step_major(dst_ref, val, *, nseq, steps, bb):
    for s in range(nseq):
        base = (s // bb) * (bb * steps) + s % bb
        for k in range(val.shape[-1] // LANES):
            dst_ref[k, pl.ds(base, steps, stride=bb), :] = (
                val[s * steps:(s + 1) * steps, k * LANES:(k + 1) * LANES])


def _ffn_in_rows(x_ref, n1_ref, wg_ref, wu_ref, wd_ref, n2_ref, win_ref, x1_ref, uc_ref, su_ref, *, bb):
    nseq, steps, _ = x_ref.shape
    x = x_ref[...].reshape(nseq * steps, D_MODEL)
    hb, inv_rms = _rms_split(x, n1_ref[...])
    x1 = x + 0.5 * _swiglu(hb, inv_rms, wg_ref, wu_ref, wd_ref)
    if x1_ref is not None:
        x1_ref[...] = x1.reshape(nseq, steps, D_MODEL)
    h2, inv_rms2 = _rms_split(x1, n2_ref[...])
    proj = jnp.dot(h2, win_ref[...], preferred_element_type=F32) * inv_rms2
    cv = proj[:, :D_CONV]
    cg = proj[:, D_CONV:2 * D_CONV]
    _store_step_major(uc_ref, cv * jax.nn.sigmoid(cg), nseq=nseq, steps=steps, bb=bb)
    _store_step_major(su_ref, proj[:, 2 * D_CONV:], nseq=nseq, steps=steps, bb=bb)


def _ffn_in_kernel(x_ref, *refs, bb):
    _ffn_in_rows(x_ref, *refs, bb=bb)


def _ffn_in(x, weights, *, seqs, bb):
    nseq, steps, _ = x.shape
    assert nseq % seqs == 0 and seqs % bb == 0
    rows = seqs * steps
    mix_spec = pl.BlockSpec((CONV_BLOCKS, rows, LANES), lambda i: (0, i, 0))
    mix_shape = jax.ShapeDtypeStruct((CONV_BLOCKS, nseq * steps, LANES), F32)
    return pl.pallas_call(
        functools.partial(_ffn_in_kernel, bb=bb),
        grid=(nseq // seqs,),
        in_specs=[pl.BlockSpec((seqs, steps, D_MODEL), lambda i: (i, 0, 0))]
                 + [_const_spec(w.shape) for w in weights],
        out_specs=[pl.BlockSpec((seqs, steps, D_MODEL), lambda i: (i, 0, 0)), mix_spec, mix_spec],
        out_shape=[jax.ShapeDtypeStruct((nseq, steps, D_MODEL), F32), mix_shape, mix_shape],
        compiler_params=pltpu.CompilerParams(dimension_semantics=("parallel",),
                                             vmem_limit_bytes=VMEM_LIMIT_BYTES),
        name="ffn_in",
    )(x, *weights)


def _ffn_in_prefixed_kernel(*refs, tiles, n_cast, bb):
    x_ref, pre_ref = refs[:2]
    cast_in = refs[2:2 + n_cast]
    weights = refs[2 + n_cast:8 + n_cast]
    x1_ref, uc_ref, su_ref, ucp_ref, sup_ref = refs[8 + n_cast:13 + n_cast]
    cast_out = refs[13 + n_cast:]
    i = pl.program_id(0)

    @pl.when(i < tiles)
    def _():
        for src, dst in zip(cast_in, cast_out):
            dst[...] = src[...].astype(BF16)
        _ffn_in_rows(x_ref, *weights, x1_ref, uc_ref, su_ref, bb=bb)

    @pl.when(i == tiles)
    def _():
        _ffn_in_rows(pre_ref, *weights, None, ucp_ref, sup_ref, bb=bb)


def _ffn_in_prefixed(x, prefix, weights, to_cast, *, steps):
    nseq, total, _ = x.shape
    pre_steps = prefix.shape[1]
    assert total % steps == 0
    tiles = total // steps
    clamp = lambda i: jnp.minimum(i, tiles - 1)
    mix_spec = pl.BlockSpec((CONV_BLOCKS, nseq * steps, LANES), lambda i: (0, clamp(i), 0))
    mix_shape = jax.ShapeDtypeStruct((CONV_BLOCKS, nseq * total, LANES), F32)
    pre_spec = pl.BlockSpec((CONV_BLOCKS, nseq * pre_steps, LANES), lambda i: (0, 0, 0))
    pre_shape = jax.ShapeDtypeStruct((CONV_BLOCKS, nseq * pre_steps, LANES), F32)
    slab_specs = []
    for w in to_cast:
        assert w.shape[0] % (tiles * 2 * SUBLANES) == 0, w.shape
        slab_specs.append(pl.BlockSpec((w.shape[0] // tiles, w.shape[1]), lambda i: (clamp(i), 0)))
    return pl.pallas_call(
        functools.partial(_ffn_in_prefixed_kernel, tiles=tiles, n_cast=len(to_cast), bb=nseq),
        grid=(tiles + 1,),
        in_specs=[pl.BlockSpec((nseq, steps, D_MODEL), lambda i: (0, clamp(i), 0)),
                  pl.BlockSpec(prefix.shape, lambda i: (0, 0, 0))] + slab_specs
                 + [_const_spec(w.shape) for w in weights],
        out_specs=[pl.BlockSpec((nseq, steps, D_MODEL), lambda i: (0, clamp(i), 0)),
                   mix_spec, mix_spec, pre_spec, pre_spec] + slab_specs,
        out_shape=[jax.ShapeDtypeStruct((nseq, total, D_MODEL), F32), mix_shape, mix_shape,
                   pre_shape, pre_shape] + [jax.ShapeDtypeStruct(w.shape, BF16) for w in to_cast],
        compiler_params=pltpu.CompilerParams(dimension_semantics=("arbitrary",),
                                             vmem_limit_bytes=VMEM_LIMIT_BYTES),
        name="ffn_in_prefixed",
    )(x, prefix, *to_cast, *weights)


def _ssm_param_kernel(lre_ref, lim_ref, ldt_ref, bre_ref, bim_ref, are_ref, aim_ref, wb_ref):
    lre = lre_ref[...]
    lim = lim_ref[...]
    dt = jnp.exp(ldt_ref[...])
    mag = jnp.exp(lre * dt)
    abr = mag * jnp.cos(lim * dt)
    abi = mag * jnp.sin(lim * dt)
    are_ref[...] = abr
    aim_ref[...] = abi
    nr = abr - 1.0
    den = lre * lre + lim * lim
    cr = (nr * lre + abi * lim) / den
    ci = (abi * lre - nr * lim) / den
    for c in range(SSM_BLOCKS):
        sl = slice(c * BLOCK_STATES, (c + 1) * BLOCK_STATES)
        crc, cic = cr[:, sl], ci[:, sl]
        br, bi = bre_ref[c], bim_ref[c]
        wb_ref[c, :, :BLOCK_STATES] = (crc * br - cic * bi).astype(BF16)
        wb_ref[c, :, BLOCK_STATES:] = (crc * bi + cic * br).astype(BF16)


def _ssm_params(lre, lim, ldt, bre, bim):
    return pl.pallas_call(
        _ssm_param_kernel,
        out_shape=[jax.ShapeDtypeStruct((1, N_STATE), F32),
                   jax.ShapeDtypeStruct((1, N_STATE), F32),
                   jax.ShapeDtypeStruct((SSM_BLOCKS, LANES, 2 * BLOCK_STATES), BF16)],
        name="ssm_params",
    )(lre, lim, ldt, bre, bim)


def _load_history(tail_ref, cs0_ref, *, bb):
    for k in range(CONV_BLOCKS):
        tail_ref[k] = cs0_ref[:, :, k * LANES:(k + 1) * LANES].reshape(CONV_TAIL * bb, LANES)


def _advance_history(tail_ref, uc_ref, rows, *, bb):
    tail_rows = CONV_TAIL * bb
    if rows >= tail_rows:
        tail_ref[...] = uc_ref[:, rows - tail_rows:rows, :]
    else:
        tail_ref[:, 0:tail_rows - rows, :] = tail_ref[:, rows:tail_rows, :]
        tail_ref[:, tail_rows - rows:tail_rows, :] = uc_ref[...]


def _store_history(cs_ref, tail_ref, *, bb):
    for k in range(CONV_BLOCKS):
        cs_ref[:, :, k * LANES:(k + 1) * LANES] = tail_ref[k].reshape(CONV_TAIL, bb, LANES)


def _zero_after(tile):
    bits = lax.shift_right_logical(lax.shift_right_logical(tile.astype(jnp.int32), 16), 16)
    return bits.astype(F32)


def _ordered_after(x, zero):
    return x + jnp.concatenate([zero] * (x.shape[-1] // LANES), axis=-1)


def _project(c, su, hbuf_ref, wb_ref):
    hbuf_ref[...] = jnp.dot(su.astype(BF16), wb_ref[c], preferred_element_type=F32)
    return hbuf_ref[0:SUBLANES, 0:LANES]


def _ssm_block(c, su, hbuf_ref, hout_ref, are_ref, aim_ref, wb_ref, hre_ref, him_ref, *, bb, steps,
               side_work=()):
    st = slice(c * BLOCK_STATES, (c + 1) * BLOCK_STATES)
    if su is not None:
        _project(c, su, hbuf_ref, wb_ref)
    ar = jnp.broadcast_to(are_ref[:, st], (SUBLANES, BLOCK_STATES))
    ai = jnp.broadcast_to(aim_ref[:, st], (SUBLANES, BLOCK_STATES))
    seq_blocks = bb // SUBLANES
    seqs = [slice(rb * SUBLANES, (rb + 1) * SUBLANES) for rb in range(seq_blocks)]
    h = [(hre_ref[s, st], him_ref[s, st]) for s in seqs]
    side_work = list(side_work)
    pending = None
    for tile in range(steps * seq_blocks):
        rb = tile % seq_blocks
        rows = slice(tile * SUBLANES, (tile + 1) * SUBLANES)
        hr, hi = h[rb]
        if side_work and tile % SCAN_PIECE_TILES == 0:
            hr = _ordered_after(hr, _zero_after(side_work.pop(0)()))
        nr = ar * hr - ai * hi + hbuf_ref[rows, :BLOCK_STATES]
        ni = ar * hi + ai * hr + hbuf_ref[rows, BLOCK_STATES:]
        h[rb] = (nr, ni)
        if hout_ref is None:
            continue
        if pending is None:
            pending = (nr, ni)
        else:
            pair = slice((tile - 1) * SUBLANES, (tile + 1) * SUBLANES)
            hout_ref[pair, :BLOCK_STATES] = jnp.concatenate([pending[0], nr], axis=0).astype(BF16)
            hout_ref[pair, BLOCK_STATES:] = jnp.concatenate([pending[1], ni], axis=0).astype(BF16)
            pending = None
    for work in side_work:
        work()
    for s, (hr, hi) in zip(seqs, h):
        hre_ref[s, st] = hr
        him_ref[s, st] = hi


def _mixer_state_kernel(uc_ref, su_ref, cs0_ref, h0re_ref, h0im_ref, are_ref, aim_ref, wb_ref,
                        cs_ref, hre_ref, him_ref, tail_ref, hbuf_ref, *, bb, steps):
    _load_history(tail_ref, cs0_ref, bb=bb)
    _advance_history(tail_ref, uc_ref, steps * bb, bb=bb)
    _store_history(cs_ref, tail_ref, bb=bb)
    hre_ref[...] = h0re_ref[...]
    him_ref[...] = h0im_ref[...]
    for c in range(SSM_BLOCKS):
        _ssm_block(c, su_ref[c], hbuf_ref, None, are_ref, aim_ref, wb_ref, hre_ref, him_ref,
                   bb=bb, steps=steps)


def _mixer_state(uc, su, cs0, h0re, h0im, are, aim, wb):
    bb = cs0.shape[1]
    rows = uc.shape[1]
    return pl.pallas_call(
        functools.partial(_mixer_state_kernel, bb=bb, steps=rows // bb),
        out_shape=[jax.ShapeDtypeStruct((CONV_TAIL, bb, D_CONV), F32),
                   jax.ShapeDtypeStruct((bb, N_STATE), F32),
                   jax.ShapeDtypeStruct((bb, N_STATE), F32)],
        scratch_shapes=[pltpu.VMEM((CONV_BLOCKS, CONV_TAIL * bb, LANES), F32),
                        pltpu.VMEM((rows, 2 * BLOCK_STATES), F32)],
        name="mixer_state",
    )(uc, su, cs0, h0re, h0im, are, aim, wb)


N_MIXER_INPUTS = 24


def _mixer_ffn_kernel(*refs, subs, bb, steps, chunks, groups):
    group, step = pl.program_id(0), pl.program_id(1)
    chunk_fn = functools.partial(_mixer_ffn_chunk, bb=bb, steps=steps, chunks=chunks, groups=groups,
                                 group=group)
    if subs == 1:
        chunk_fn(*refs, chunk=step)
        return
    rows = steps * bb

    def sub(s, carry):
        row0 = pl.multiple_of(s * rows, rows)
        step0 = pl.multiple_of(s * steps, steps)
        view = list(refs)
        for i in (0, 1):
            view[i] = refs[i].at[:, pl.ds(row0, rows), :]
        for i in (2, N_MIXER_INPUTS):
            view[i] = refs[i].at[:, pl.ds(step0, steps), :]
        chunk_fn(*view, chunk=step * subs + s)
        return carry

    lax.fori_loop(0, subs, sub, 0)


def _mixer_ffn_chunk(uc_ref, su_ref, x1_ref, cs0_ref, h0re_ref, h0im_ref,
                     cw_ref, cb_ref, lg_ref, lb_ref, are_ref, aim_ref, wb_ref, wcre_ref, wcim_ref,
                     d_ref, gw_ref, gb_ref, wo_ref,
                     n2_ref, wg_ref, wu_ref, wd_ref, nf_ref,
                     y_ref, cs_ref, hre_ref, him_ref,
                     tail_ref, hbuf_ref, hout_ref, yc_ref, mix_ref, x2_ref, *state_scratch,
                     bb, steps, chunks, groups, group, chunk):
    rows = steps * bb
    tail_rows = CONV_TAIL * bb
    if state_scratch:
        all_re, all_im = state_scratch
        first = pl.multiple_of(group * bb, bb)
        state_re, state_im = all_re.at[pl.ds(first, bb), :], all_im.at[pl.ds(first, bb), :]

        @pl.when(jnp.logical_and(group == 0, chunk == 0))
        def _():
            all_re[...] = h0re_ref[...].T
            all_im[...] = h0im_ref[...].T
    else:
        state_re, state_im = hre_ref, him_ref

    @pl.when(chunk == 0)
    def _():
        if not state_scratch:
            hre_ref[...] = h0re_ref[...]
            him_ref[...] = h0im_ref[...]
        _load_history(tail_ref, cs0_ref, bb=bb)

    def history_or_chunk(k, row):
        if row < tail_rows:
            return tail_ref[k, row:row + SUBLANES, :]
        return uc_ref[k, row - tail_rows:row - tail_rows + SUBLANES, :]

    def conv_unit(r, k):
        o = r * CONV_ROW_GROUP
        cols = slice(k * LANES, (k + 1) * LANES)
        accs = [jnp.zeros((SUBLANES, LANES), F32)] * (CONV_ROW_GROUP // SUBLANES)
        for tap in range(CONV_K):
            wt = cw_ref[tap:tap + 1, cols]
            accs = [a + wt * history_or_chunk(k, o + q * SUBLANES + tap * bb)
                    for q, a in enumerate(accs)]
        for q, a in enumerate(accs):
            yc_ref[k, o + q * SUBLANES:o + (q + 1) * SUBLANES, :] = a + cb_ref[:, cols]
        return functools.reduce(jnp.add, accs)

    conv = [functools.partial(conv_unit, r, k)
            for r in range(rows // CONV_ROW_GROUP) for k in range(CONV_BLOCKS)]
    per_block = len(conv) // SSM_BLOCKS

    def with_projection(work, c):
        return lambda: work() + _project(c, su_ref[c], hbuf_ref.at[c % 2], wb_ref)

    zs = []
    def ssm_out(c):
        hb = hout_ref[c % 2]
        y = (jnp.dot(hb[:, :BLOCK_STATES], wcre_ref[c], preferred_element_type=F32)
             - jnp.dot(hb[:, BLOCK_STATES:], wcim_ref[c], preferred_element_type=F32))
        zs.append(jax.nn.gelu(y + d_ref[:, c * LANES:(c + 1) * LANES] * su_ref[c]))
        return zs[-1][0:SUBLANES]

    def with_output(work, c):
        return lambda: work() + ssm_out(c)

    _project(0, su_ref[0], hbuf_ref.at[0], wb_ref)
    for c in range(SSM_BLOCKS):
        work = conv[c * per_block:(c + 1) * per_block]
        if c > 0:
            work[1] = with_output(work[1], c - 1)
        if c + 1 < SSM_BLOCKS:
            work[per_block // 2] = with_projection(work[per_block // 2], c + 1)
        _ssm_block(c, None, hbuf_ref.at[c % 2], hout_ref.at[c % 2], are_ref, aim_ref, wb_ref,
                   state_re, state_im, bb=bb, steps=steps, side_work=work)
    ssm_out(SSM_BLOCKS - 1)

    _advance_history(tail_ref, uc_ref, rows, bb=bb)

    ycs = [yc_ref[k] for k in range(CONV_BLOCKS)]
    mu = sum(jnp.sum(y, axis=-1, keepdims=True) for y in ycs) * (1.0 / D_CONV)
    cens = [y - mu for y in ycs]
    var = sum(jnp.sum(c * c, axis=-1, keepdims=True) for c in cens) * (1.0 / D_CONV)
    inv = lax.rsqrt(var + LN_EPS)
    acts = []
    for k, cen in enumerate(cens):
        cols = slice(k * LANES, (k + 1) * LANES)
        yn = cen * inv * lg_ref[:, cols] + lb_ref[:, cols]
        acts.append((yn * jax.nn.sigmoid(yn)).astype(BF16))

    z = jnp.concatenate(zs, axis=-1)
    gate = jnp.dot(z.astype(BF16), gw_ref[...], preferred_element_type=F32) + gb_ref[...]
    outs = (z * jax.nn.sigmoid(gate)).astype(BF16)

    mix = jnp.dot(jnp.concatenate(acts + [outs], axis=-1), wo_ref[...], preferred_element_type=F32)
    for k in range(MODEL_BLOCKS):
        mix_ref[k] = mix[:, k * LANES:(k + 1) * LANES]
    for b in range(bb):
        for k in range(MODEL_BLOCKS):
            cols = slice(k * LANES, (k + 1) * LANES)
            x2_ref[b * steps:(b + 1) * steps, cols] = (
                x1_ref[b, :, cols] + mix_ref[k, pl.ds(b, steps, stride=bb), :])

    x2 = x2_ref[...]
    hb2, inv_rms2 = _rms_split(x2, n2_ref[...])
    x3 = x2 + 0.5 * _swiglu(hb2, inv_rms2, wg_ref, wu_ref, wd_ref)
    y_ref[...] = _rmsnorm(x3, nf_ref[...]).reshape(bb, steps, D_MODEL)

    @pl.when(chunk == chunks - 1)
    def _():
        _store_history(cs_ref, tail_ref, bb=bb)

    if state_scratch:
        @pl.when(jnp.logical_and(group == groups - 1, chunk == chunks - 1))
        def _():
            hre_ref[...] = all_re[...].T
            him_ref[...] = all_im[...].T


def _mixer_ffn(uc, su, x1, cs0, h0re, h0im, consts, ffn, *, bb, steps, subs=1):
    nseq, total, _ = x1.shape
    groups, chunks = nseq // bb, total // steps
    rows, tail_rows = steps * bb, CONV_TAIL * bb
    assert nseq % bb == 0 and total % steps == 0 and bb % SUBLANES == 0 and steps % SUBLANES == 0
    assert rows % CONV_ROW_GROUP == 0 and (groups == 1 or chunks == 1) and chunks % subs == 0
    state_major = h0re.shape == (N_STATE, nseq)
    assert state_major or h0re.shape == (nseq, N_STATE)
    grid_chunks = chunks // subs
    seq_spec = pl.BlockSpec((bb, steps * subs, D_MODEL), lambda g, i: (g, i, 0))
    mix_spec = pl.BlockSpec((CONV_BLOCKS, rows * subs, LANES), lambda g, i: (0, g * grid_chunks + i, 0))
    if state_major:
        state_spec = pl.BlockSpec((N_STATE, nseq), lambda g, i: (0, 0))
        state_scratch = [pltpu.VMEM((nseq, N_STATE), F32)] * 2
    else:
        state_spec = pl.BlockSpec((bb, N_STATE), lambda g, i: (g, 0))
        state_scratch = []
    cs_spec = pl.BlockSpec((CONV_TAIL, bb, D_CONV), lambda g, i: (0, g, 0))
    return pl.pallas_call(
        functools.partial(_mixer_ffn_kernel, subs=subs, bb=bb, steps=steps, chunks=chunks,
                          groups=groups),
        grid=(groups, grid_chunks),
        in_specs=[mix_spec, mix_spec, seq_spec, cs_spec, state_spec, state_spec]
                 + [_const_spec(c.shape) for c in consts + ffn],
        out_specs=[seq_spec, cs_spec, state_spec, state_spec],
        out_shape=[jax.ShapeDtypeStruct((nseq, total, D_MODEL), F32),
                   jax.ShapeDtypeStruct((CONV_TAIL, nseq, D_CONV), F32),
                   jax.ShapeDtypeStruct(h0re.shape, F32),
                   jax.ShapeDtypeStruct(h0re.shape, F32)],
        scratch_shapes=[pltpu.VMEM((CONV_BLOCKS, tail_rows, LANES), F32),
                        pltpu.VMEM((2, rows, 2 * BLOCK_STATES), F32),
                        pltpu.VMEM((2, rows, 2 * BLOCK_STATES), BF16),
                        pltpu.VMEM((CONV_BLOCKS, rows, LANES), F32),
                        pltpu.VMEM((MODEL_BLOCKS, rows, LANES), F32),
                        pltpu.VMEM((rows, D_MODEL), F32)] + state_scratch,
        compiler_params=pltpu.CompilerParams(dimension_semantics=("arbitrary", "arbitrary"),
                                             vmem_limit_bytes=VMEM_LIMIT_BYTES),
        name="mixer_ffn",
    )(uc, su, x1, cs0, h0re, h0im, *consts, *ffn)


def _block_diag(blocks):
    nb, n, a, b = blocks.shape
    eye = jnp.eye(n, dtype=blocks.dtype)
    return jnp.einsum("cgab,gh->cgahb", blocks, eye).reshape(nb, n * a, n * b)


PROMPT_STEPS = 64
PROMPT_SUBS = 2
PROMPT_FFN_STEPS = 128
SAMPLE_SEQS = 32
SAMPLE_FFN_SEQS = 64


def kernel(x_prompt, x_sample, state_conv, state_ssm_re, state_ssm_im, meta_tokens, ffn1_norm, ffn1_w_gate, ffn1_w_up, ffn1_w_down, mix_norm, w_in, conv_w, conv_b, conv_ln_g, conv_ln_b, ssm_lambda_re, ssm_lambda_im, ssm_log_dt, ssm_b_re, ssm_b_im, ssm_c_re, ssm_c_im, ssm_d, ssm_glu_w, ssm_glu_b, w_out, ffn2_norm, ffn2_w_gate, ffn2_w_up, ffn2_w_down, final_norm):
    assert ffn1_norm.shape[0] == 1, "one layer"
    bp, lp, _ = x_prompt.shape
    bs, ls, _ = x_sample.shape
    groups_per_block = SSM_GROUPS // SSM_BLOCKS

    row = lambda v: v.reshape(1, -1).astype(F32)
    w1 = (row(ffn1_norm[0]), ffn1_w_gate[0].astype(BF16), ffn1_w_up[0].astype(BF16),
          ffn1_w_down[0].astype(BF16), row(mix_norm[0]), w_in[0].astype(BF16))
    b_blocks = lambda b: _block_diag(
        b.reshape(SSM_BLOCKS, groups_per_block, SSM_P, SSM_GC).transpose(0, 1, 3, 2))
    c_blocks = lambda c: _block_diag(
        c.reshape(SSM_BLOCKS, groups_per_block, SSM_GC, SSM_P).transpose(0, 1, 3, 2)).astype(BF16)
    are, aim, wb = _ssm_params(row(ssm_lambda_re[0]), row(ssm_lambda_im[0]),
                               row(jnp.repeat(ssm_log_dt[0], SSM_P)),
                               b_blocks(ssm_b_re[0]), b_blocks(ssm_b_im[0]))

    meta = jnp.broadcast_to(meta_tokens.astype(x_prompt.dtype)[None], (bp, N_META, D_MODEL))
    x1p, ucp, sup, ucm, sum_, w2g, w2u, w2d, wo, gw = _ffn_in_prefixed(
        x_prompt, meta, w1, (ffn2_w_gate[0], ffn2_w_up[0], ffn2_w_down[0], w_out[0], ssm_glu_w[0]),
        steps=PROMPT_FFN_STEPS)
    w2 = (row(ffn2_norm[0]), w2g, w2u, w2d, row(final_norm))
    consts = (conv_w[0], row(conv_b[0]), row(conv_ln_g[0]), row(conv_ln_b[0]), are, aim, wb,
              c_blocks(ssm_c_re[0]), c_blocks(ssm_c_im[0]), row(ssm_d[0]),
              gw, row(ssm_glu_b[0]), wo)
    cs_m, hre_m, him_m = _mixer_state(
        ucm, sum_, jnp.zeros((CONV_TAIL, bp, D_CONV), F32), jnp.zeros((bp, N_STATE), F32),
        jnp.zeros((bp, N_STATE), F32), are, aim, wb)
    y_prompt, cs_p, hre_p, him_p = _mixer_ffn(ucp, sup, x1p, cs_m, hre_m, him_m, consts, w2,
                                              bb=bp, steps=PROMPT_STEPS, subs=PROMPT_SUBS)

    x1s, ucs, sus = _ffn_in(x_sample, w1, seqs=SAMPLE_FFN_SEQS, bb=SAMPLE_SEQS)
    state_in = lambda h: h[0].transpose(1, 2, 0).reshape(N_STATE, bs)
    state_out = lambda h: h.reshape(SSM_GROUPS, SSM_P, bs).transpose(2, 0, 1)[None]
    y_sample, cs_s, hre_s, him_s = _mixer_ffn(
        ucs, sus, x1s, state_conv[0].transpose(1, 0, 2), state_in(state_ssm_re),
        state_in(state_ssm_im), consts, w2, bb=SAMPLE_SEQS, steps=ls)

    state = lambda h, b: h.reshape(1, b, SSM_GROUPS, SSM_P)
    history = lambda cs: cs.transpose(1, 0, 2)[None]
    return (y_prompt, y_sample, history(cs_p), state(hre_p, bp), state(him_p, bp),
            history(cs_s), state_out(hre_s), state_out(him_s))
```
